```python
import math
import jax, jax.numpy as jnp
from jax import lax
import numpy as np


D_MODEL = 1024
BATCH = 2
SEQ = 16384
DEPTH = 1

MIX_WIDTH = D_MODEL
HG_WIDTH = MIX_WIDTH // 2
HG_HEADS = 4
HG_DK = HG_WIDTH // HG_HEADS
HG_CHUNK = 64
S5_WIDTH = MIX_WIDTH - HG_WIDTH
S5_GROUP = 16
S5_GROUPS = S5_WIDTH // S5_GROUP
S5_STATE = 64
IN_WIDTH = 4 * HG_WIDTH + S5_WIDTH
PEER_HEADS = 8
PEER_NKEYS = 128
PEER_EXPERTS = PEER_NKEYS * PEER_NKEYS
PEER_TOPK = 16
PEER_QDIM = 256
PEER_HALF = PEER_QDIM // 2
PEER_TOKEN_BLOCK = 128
EPS = 1e-6

kernel_name = 'hymba_hgrn2_s5_peer_adaln'


def _rmsnorm(x, w):
    xf = x.astype(jnp.float32)
    y = xf * lax.rsqrt(jnp.mean(xf * xf, axis=-1, keepdims=True) + EPS)
    return (y * w.astype(jnp.float32)).astype(x.dtype)


def _hgrn2(q, f_logit, i, g, lb, gn_w):
    bsz, seq, _ = q.shape
    f32 = jnp.float32
    f = lb + (1.0 - lb) * jax.nn.sigmoid(f_logit.astype(f32))
    log_f = jnp.log(f)
    k = 1.0 - f
    qs = jax.nn.silu(q.astype(f32)) * (HG_DK ** -0.5)
    v = i.astype(f32)
    n_chunks = seq // HG_CHUNK

    def to_chunks(t):
        return t.reshape(bsz, n_chunks, HG_CHUNK, HG_HEADS, HG_DK).transpose(1, 0, 3, 2, 4)

    causal = jnp.tril(jnp.ones((HG_CHUNK, HG_CHUNK), dtype=bool))[:, :, None]

    def step(state, inp):
        qc, kc, vc, lc = inp
        bcum = jnp.cumsum(lc, axis=2)
        o_inter = jnp.einsum('bhtk,bhkv->bhtv', qc * jnp.exp(bcum), state)
        diff = bcum[:, :, :, None, :] - bcum[:, :, None, :, :]
        decay = jnp.exp(jnp.where(causal, diff, -jnp.inf))
        scores = jnp.einsum('bhtk,bhsk,bhtsk->bhts', qc, kc, decay)
        o_intra = jnp.einsum('bhts,bhsv->bhtv', scores, vc)
        b_last = bcum[:, :, -1:, :]
        new_state = (jnp.exp(b_last[:, :, 0, :])[..., None] * state
                     + jnp.einsum('bhsk,bhsv->bhkv', kc * jnp.exp(b_last - bcum), vc))
        return new_state, o_inter + o_intra

    s0 = jnp.zeros((bsz, HG_HEADS, HG_DK, HG_DK), f32)
    _, o = lax.scan(step, s0, (to_chunks(qs), to_chunks(k), to_chunks(v), to_chunks(log_f)))
    o = o.transpose(1, 0, 3, 2, 4).reshape(bsz, seq, HG_HEADS, HG_DK)
    gate = jax.nn.silu(g.astype(f32)).reshape(bsz, seq, HG_HEADS, HG_DK)
    o = _rmsnorm(o, gn_w) * gate
    return o.reshape(bsz, seq, HG_WIDTH)


def _s5(u, a_re, a_im, log_dt, b_re, b_im, c_re, c_im, d_skip, w_glu, b_glu):
    bsz, seq, _ = u.shape
    f32 = jnp.float32
    uf = u.astype(f32).reshape(bsz, seq, S5_GROUPS, S5_GROUP)
    ar = a_re.astype(f32)
    ai = a_im.astype(f32)
    dt = jnp.exp(log_dt.astype(f32))[:, None]
    mag = jnp.exp(ar * dt)
    abar_re = mag * jnp.cos(ai * dt)
    abar_im = mag * jnp.sin(ai * dt)
    den = ar * ar + ai * ai
    nr = abar_re - 1.0
    ni = abar_im
    coef_re = ((nr * ar + ni * ai) / den)[..., None]
    coef_im = ((ni * ar - nr * ai) / den)[..., None]
    br = b_re.astype(f32)
    bi = b_im.astype(f32)
    bb_re = coef_re * br - coef_im * bi
    bb_im = coef_re * bi + coef_im * br
    bu_re = jnp.einsum('bsgh,gph->bsgp', uf, bb_re)
    bu_im = jnp.einsum('bsgh,gph->bsgp', uf, bb_im)
    a_re_t = jnp.broadcast_to(abar_re, bu_re.shape)
    a_im_t = jnp.broadcast_to(abar_im, bu_im.shape)

    def combine(e1, e2):
        a1r, a1i, x1r, x1i = e1
        a2r, a2i, x2r, x2i = e2
        return (a2r * a1r - a2i * a1i,
                a2r * a1i + a2i * a1r,
                a2r * x1r - a2i * x1i + x2r,
                a2r * x1i + a2i * x1r + x2i)

    _, _, xr, xi = lax.associative_scan(combine, (a_re_t, a_im_t, bu_re, bu_im), axis=1)
    y = (jnp.einsum('bsgp,ghp->bsgh', xr, c_re.astype(f32))
         - jnp.einsum('bsgp,ghp->bsgh', xi, c_im.astype(f32)))
    y = y.reshape(bsz, seq, S5_WIDTH) + d_skip.astype(f32) * uf.reshape(bsz, seq, S5_WIDTH)
    y = jax.nn.gelu(y, approximate=False)
    y = y * jax.nn.sigmoid(y @ w_glu.astype(f32) + b_glu.astype(f32))
    return y.astype(u.dtype)


def _peer(h, w_q, keys1, keys2, u_tab, v_tab):
    bsz, seq, d = h.shape
    tb = PEER_TOKEN_BLOCK
    blocks = h.reshape(-1, tb, d)
    kk = PEER_TOPK * PEER_TOPK

    def block(hb):
        qh = (hb @ w_q).reshape(tb, PEER_HEADS, 2, PEER_HALF)
        s1 = jnp.einsum('thd,nd->thn', qh[:, :, 0], keys1)
        s2 = jnp.einsum('thd,nd->thn', qh[:, :, 1], keys2)
        v1, i1 = lax.top_k(s1, PEER_TOPK)
        v2, i2 = lax.top_k(s2, PEER_TOPK)
        cand = (v1[..., :, None] + v2[..., None, :]).reshape(tb, PEER_HEADS, kk)
        cidx = (i1[..., :, None] * PEER_NKEYS + i2[..., None, :]).reshape(tb, PEER_HEADS, kk)
        best, pos = lax.top_k(cand, PEER_TOPK)
        eidx = jnp.take_along_axis(cidx, pos, axis=-1)
        gates = jax.nn.softmax(best.astype(jnp.float32), axis=-1).astype(hb.dtype)
        act = jax.nn.gelu(jnp.einsum('td,thkd->thk', hb, u_tab[eidx]), approximate=False)
        return jnp.einsum('thk,thkd->td', gates * act, v_tab[eidx])

    return lax.map(block, blocks).reshape(bsz, seq, d)


def setup_inputs(seed: int = 0) -> dict:
    key = jax.random.key(seed)
    ks = jax.random.split(key, 26)
    f32 = jnp.float32

    def nrm(k, shape, std):
        return std * jax.random.normal(k, shape, f32)

    L = DEPTH
    D = D_MODEL
    G = S5_GROUPS
    P = S5_STATE
    return {
        'x': nrm(ks[0], (BATCH, SEQ, D), 1.0),
        'c': nrm(ks[1], (BATCH, D), 1.0),
        'ada_w': nrm(ks[2], (L, D, 6 * D), 0.5 * D ** -0.5),
        'ada_b': nrm(ks[3], (L, 6 * D), 0.01),
        'norm_mix_w': 1.0 + nrm(ks[4], (L, D), 0.02),
        'norm_ffn_w': 1.0 + nrm(ks[5], (L, D), 0.02),
        'w_in': nrm(ks[6], (L, D, IN_WIDTH), D ** -0.5),
        'w_out': nrm(ks[7], (L, MIX_WIDTH, D), MIX_WIDTH ** -0.5),
        'hg_lower_bounds': nrm(ks[8], (L + 1, HG_WIDTH), 0.1),
        'hg_gnorm_w': 1.0 + nrm(ks[9], (L, HG_DK), 0.02),
        's5_a_re': -0.5 + nrm(ks[10], (L, G, P), 0.01),
        's5_a_im': math.pi * jnp.arange(P, dtype=f32) + nrm(ks[11], (L, G, P), 0.01),
        's5_log_dt': jax.random.uniform(ks[12], (L, G), f32, math.log(1e-3), math.log(1e-1)),
        's5_b_re': nrm(ks[13], (L, G, P, S5_GROUP), (2 * S5_GROUP) ** -0.5),
        's5_b_im': nrm(ks[14], (L, G, P, S5_GROUP), (2 * S5_GROUP) ** -0.5),
        's5_c_re': nrm(ks[15], (L, G, S5_GROUP, P), 0.25),
        's5_c_im': nrm(ks[16], (L, G, S5_GROUP, P), 0.25),
        's5_d': nrm(ks[17], (L, S5_WIDTH), 0.5),
        's5_glu_w': nrm(ks[18], (L, S5_WIDTH, S5_WIDTH), S5_WIDTH ** -0.5),
        's5_glu_b': nrm(ks[19], (L, S5_WIDTH), 0.01),
        'peer_wq': nrm(ks[20], (L, D, PEER_HEADS * PEER_QDIM), D ** -0.5),
        'peer_keys1': nrm(ks[21], (L, PEER_NKEYS, PEER_HALF), PEER_HALF ** -0.5),
        'peer_keys2': nrm(ks[22], (L, PEER_NKEYS, PEER_HALF), PEER_HALF ** -0.5),
        'peer_u': nrm(ks[23], (L, PEER_EXPERTS, D), D ** -0.5),
        'peer_v': nrm(ks[24], (L, PEER_EXPERTS, D), PEER_HEADS ** -0.5),
        'final_norm_w': 1.0 + nrm(ks[25], (D,), 0.02),
    }


def reference(x, c, ada_w, ada_b, norm_mix_w, norm_ffn_w, w_in, w_out, hg_lower_bounds,
              hg_gnorm_w, s5_a_re, s5_a_im, s5_log_dt, s5_b_re, s5_b_im, s5_c_re, s5_c_im,
              s5_d, s5_glu_w, s5_glu_b, peer_wq, peer_keys1, peer_keys2, peer_u, peer_v,
              final_norm_w):
    cond = jax.nn.silu(c)
    lb_all = jnp.cumsum(jax.nn.softmax(hg_lower_bounds.astype(jnp.float32), axis=0), axis=0)
    for l in range(DEPTH):
        mod = (cond @ ada_w[l] + ada_b[l])[:, None, :]
        sh1, sc1, gt1, sh2, sc2, gt2 = jnp.split(mod, 6, axis=-1)
        h = _rmsnorm(x, norm_mix_w[l]) * (1.0 + sc1) + sh1
        proj = h @ w_in[l]
        q, f_logit, i, g, u = jnp.split(
            proj, [HG_WIDTH, 2 * HG_WIDTH, 3 * HG_WIDTH, 4 * HG_WIDTH], axis=-1)
        o_hg = _hgrn2(q, f_logit, i, g, lb_all[l], hg_gnorm_w[l]).astype(x.dtype)
        o_s5 = _s5(u, s5_a_re[l], s5_a_im[l], s5_log_dt[l], s5_b_re[l], s5_b_im[l],
                   s5_c_re[l], s5_c_im[l], s5_d[l], s5_glu_w[l], s5_glu_b[l])
        mixed = jnp.concatenate([o_hg, o_s5], axis=-1) @ w_out[l]
        x = x + gt1 * mixed
        h2 = _rmsnorm(x, norm_ffn_w[l]) * (1.0 + sc2) + sh2
        x = x + gt2 * _peer(h2, peer_wq[l], peer_keys1[l], peer_keys2[l], peer_u[l], peer_v[l])
    return _rmsnorm(x, final_norm_w)
```

```python
import functools
import math

import jax
import jax.numpy as jnp
from jax import lax
from jax.experimental import pallas as pl
from jax.experimental.pallas import tpu as pltpu

F32 = jnp.float32
BF16 = jnp.bfloat16
I32 = jnp.int32

EPS = 1e-6
HG_HEADS = 4
HG_DK = 128
HG_CHUNK = 64
S5_GROUP = 16
S5_STATE = 64
PEER_HEADS = 8
PEER_NKEYS = 128
PEER_TOPK = 16
PEER_HALF = 128

LANES = 128
SUBLANES = 8
VMEM_LIMIT = 56 * 1024 * 1024

_HIGHEST = lax.Precision.HIGHEST
_SQRT_HALF = 0.7071067811865476


def _cparams(*sem, vmem=None):
    return pltpu.CompilerParams(dimension_semantics=sem, vmem_limit_bytes=vmem)


def _silu(x):
    return x * jax.nn.sigmoid(x)


def _gelu(x):
    return 0.5 * x * (1.0 + lax.erf(x * _SQRT_HALF))


def _bdot(a, b):
    return jnp.dot(a.astype(BF16), b.astype(BF16), preferred_element_type=F32)


def _mod_kernel(c_ref, w_ref, b_ref, o_ref):
    cond = _silu(c_ref[...])
    o_ref[...] = jnp.dot(cond, w_ref[...], preferred_element_type=F32, precision=_HIGHEST) + b_ref[...]


def _mod(c, w, b):
    bsz, d = c.shape
    n = w.shape[1]
    return pl.pallas_call(
        _mod_kernel,
        grid=(n // d,),
        in_specs=[pl.BlockSpec((bsz, d), lambda j: (0, 0)),
                  pl.BlockSpec((d, d), lambda j: (0, j)),
                  pl.BlockSpec((1, d), lambda j: (0, j))],
        out_specs=pl.BlockSpec((bsz, d), lambda j: (0, j)),
        out_shape=jax.ShapeDtypeStruct((bsz, n), F32),
        compiler_params=_cparams("arbitrary"),
        name="mod",
    )(c, w, b.reshape(1, n))


def _norm_mod(x, nw, scale, shift):
    ms = jnp.mean(x * x, axis=-1, keepdims=True)
    return (x * lax.rsqrt(ms + EPS) * nw) * (1.0 + scale) + shift


def _inproj_kernel(x_ref, mod_ref, nw_ref, w_ref, o_ref):
    h = _norm_mod(x_ref[0], nw_ref[...], mod_ref[0, 1:2, :], mod_ref[0, 0:1, :])
    o_ref[0] = jnp.dot(h.astype(BF16), w_ref[...], preferred_element_type=F32)


def _inproj(x, mod6, nw, w_bf16, tm):
    bsz, seq, d = x.shape
    n = w_bf16.shape[1]
    return pl.pallas_call(
        _inproj_kernel,
        grid=(bsz, seq // tm),
        in_specs=[pl.BlockSpec((1, tm, d), lambda b, i: (b, i, 0)),
                  pl.BlockSpec((1, 6, d), lambda b, i: (b, 0, 0)),
                  pl.BlockSpec((1, d), lambda b, i: (0, 0)),
                  pl.BlockSpec((d, n), lambda b, i: (0, 0))],
        out_specs=pl.BlockSpec((1, tm, n), lambda b, i: (b, i, 0)),
        out_shape=jax.ShapeDtypeStruct((bsz, seq, n), F32),
        compiler_params=_cparams("arbitrary", "arbitrary", vmem=VMEM_LIMIT),
        name="inproj",
    )(x, mod6, nw.reshape(1, d), w_bf16)


def _hgrn2_kernel(layer, q_ref, f_ref, i_ref, g_ref, lbp_ref, gnw_ref, o_ref,
                  st_ref, b_s, qs_s, oi_s):
    c_len = HG_CHUNK
    n_chunks = q_ref.shape[1] // c_len

    @pl.when(pl.program_id(1) == 0)
    def _():
        st_ref[...] = jnp.zeros_like(st_ref)

    lbp = lbp_ref[...]
    e = jnp.exp(lbp - jnp.max(lbp, axis=0, keepdims=True))
    lb_all = jnp.sum(e[: layer + 1], axis=0, keepdims=True) / jnp.sum(e, axis=0, keepdims=True)
    gnw = gnw_ref[...]
    row = lax.broadcasted_iota(I32, (c_len, c_len), 0)
    col = lax.broadcasted_iota(I32, (c_len, c_len), 1)
    tril = (col <= row).astype(F32)
    s_iota = lax.broadcasted_iota(I32, (c_len, 1), 0)

    def chunk(ci, carry):
        r0 = pl.multiple_of(ci * c_len, c_len)
        for h in range(HG_HEADS):
            cs = slice(h * HG_DK, (h + 1) * HG_DK)
            lb = lb_all[:, cs]
            fl = f_ref[0, pl.ds(r0, c_len), cs]
            f = lb + (1.0 - lb) * jax.nn.sigmoid(fl)
            logf = jnp.log(f)
            kk = 1.0 - f
            qs = _silu(q_ref[0, pl.ds(r0, c_len), cs]) * (HG_DK ** -0.5)
            v = i_ref[0, pl.ds(r0, c_len), cs]
            b = jnp.dot(tril, logf, preferred_element_type=F32, precision=_HIGHEST)
            st = st_ref[h]
            o_inter = lax.dot_general((qs * jnp.exp(b)).astype(BF16), st.astype(BF16),
                                      (((1,), (1,)), ((), ())), preferred_element_type=F32)
            b_s[...] = b
            qs_s[...] = qs

            def row_step(t, c2):
                bt = b_s[pl.ds(t, 1), :]
                qt = qs_s[pl.ds(t, 1), :]
                w = (qt * kk) * jnp.exp(jnp.minimum(bt - b, 0.0))
                sc = jnp.sum(w, axis=1, keepdims=True)
                sc = jnp.where(s_iota <= t, sc, 0.0)
                oi_s[pl.ds(t, 1), :] = jnp.sum(sc * v, axis=0, keepdims=True)
                return c2

            lax.fori_loop(0, c_len, row_step, 0)
            o = o_inter + oi_s[...]
            b_last = b[c_len - 1:c_len, :]
            kdec = kk * jnp.exp(b_last - b)
            upd = lax.dot_general(v.astype(BF16), kdec.astype(BF16),
                                  (((0,), (0,)), ((), ())), preferred_element_type=F32)
            st_ref[h] = st * jnp.exp(b_last) + upd
            y = o * lax.rsqrt(jnp.mean(o * o, axis=-1, keepdims=True) + EPS) * gnw
            o_ref[0, pl.ds(r0, c_len), cs] = y * _silu(g_ref[0, pl.ds(r0, c_len), cs])
        return carry

    lax.fori_loop(0, n_chunks, chunk, 0)


def _hgrn2(proj, lbp, gnw, layer, tb):
    bsz, seq, _ = proj.shape
    w = HG_HEADS * HG_DK
    nl = lbp.shape[0]

    def col_spec(j):
        return pl.BlockSpec((1, tb, w), lambda b, i, j=j: (b, i, j))

    return pl.pallas_call(
        functools.partial(_hgrn2_kernel, layer),
        grid=(bsz, seq // tb),
        in_specs=[col_spec(0), col_spec(1), col_spec(2), col_spec(3),
                  pl.BlockSpec((nl, w), lambda b, i: (0, 0)),
                  pl.BlockSpec((1, HG_DK), lambda b, i: (0, 0))],
        out_specs=pl.BlockSpec((1, tb, w), lambda b, i: (b, i, 0)),
        out_shape=jax.ShapeDtypeStruct((bsz, seq, w), F32),
        scratch_shapes=[pltpu.VMEM((HG_HEADS, HG_DK, HG_DK), F32),
                        pltpu.VMEM((HG_CHUNK, HG_DK), F32),
                        pltpu.VMEM((HG_CHUNK, HG_DK), F32),
                        pltpu.VMEM((HG_CHUNK, HG_DK), F32)],
        compiler_params=_cparams("arbitrary", "arbitrary", vmem=VMEM_LIMIT),
        name="hgrn2",
    )(proj, proj, proj, proj, lbp, gnw.reshape(1, HG_DK))


def _s5prep_kernel(are_ref, aim_ref, ldt_ref, bre_ref, bim_ref, pre_ref, pim_ref, bbre_ref, bbim_ref):
    ar = are_ref[...]
    ai = aim_ref[...]
    dt = jnp.exp(ldt_ref[...])
    mag = jnp.exp(ar * dt)
    abr = mag * jnp.cos(ai * dt)
    abi = mag * jnp.sin(ai * dt)
    den = ar * ar + ai * ai
    nr = abr - 1.0
    ni = abi
    cre = (nr * ar + ni * ai) / den
    cim = (ni * ar - nr * ai) / den
    bre = bre_ref[...]
    bim = bim_ref[...]
    bbre_ref[...] = cre[:, None, :] * bre - cim[:, None, :] * bim
    bbim_ref[...] = cre[:, None, :] * bim + cim[:, None, :] * bre
    pr, pi = abr, abi
    pre_ref[0] = pr
    pim_ref[0] = pi
    for k in range(1, SUBLANES):
        pr, pi = pr * abr - pi * abi, pr * abi + pi * abr
        pre_ref[k] = pr
        pim_ref[k] = pi


def _s5prep(a_re, a_im, log_dt, b_re, b_im):
    g, p = a_re.shape
    hh = b_re.shape[2]
    out = pl.pallas_call(
        _s5prep_kernel,
        out_shape=(jax.ShapeDtypeStruct((SUBLANES, g, p), F32), jax.ShapeDtypeStruct((SUBLANES, g, p), F32),
                   jax.ShapeDtypeStruct((g, hh, p), F32), jax.ShapeDtypeStruct((g, hh, p), F32)),
        name="s5prep",
    )(a_re, a_im, log_dt.reshape(g, 1), jnp.swapaxes(b_re, 1, 2), jnp.swapaxes(b_im, 1, 2))
    return out


S5_COLS = 256


def _s5_kernel(u_ref, w1_ref, t1r, t1i, t2r, t2i, t4r, t4i, pr_ref, pi_ref, w2r_ref, w2i_ref,
               d_ref, wg_ref, bg_ref, o_ref, xr_s, xi_s, cr_s, ci_s):
    n = xr_s.shape[1]
    rows = xr_s.shape[0]

    @pl.when(pl.program_id(1) == 0)
    def _():
        cr_s[...] = jnp.zeros_like(cr_s)
        ci_s[...] = jnp.zeros_like(ci_s)

    u = u_ref[0]
    bu = jnp.dot(u.astype(BF16), w1_ref[...], preferred_element_type=F32)
    xr_s[...] = bu[:, :n]
    xi_s[...] = bu[:, n:]

    for c0 in range(0, n, S5_COLS):
        cs = slice(c0, c0 + S5_COLS)
        steps = ((1, t1r[:, cs], t1i[:, cs]), (2, t2r[:, cs], t2i[:, cs]), (4, t4r[:, cs], t4i[:, cs]))
        pr = pr_ref[:, cs]
        pi = pi_ref[:, cs]

        def group(j, carry, cs=cs, steps=steps, pr=pr, pi=pi):
            cr, ci = carry
            r0 = pl.multiple_of(j * SUBLANES, SUBLANES)
            xr = xr_s[pl.ds(r0, SUBLANES), cs]
            xi = xi_s[pl.ds(r0, SUBLANES), cs]
            for d, ar, ai in steps:
                sr = pltpu.roll(xr, d, 0)
                si = pltpu.roll(xi, d, 0)
                xr, xi = xr + ar * sr - ai * si, xi + ar * si + ai * sr
            xr, xi = xr + pr * cr - pi * ci, xi + pr * ci + pi * cr
            xr_s[pl.ds(r0, SUBLANES), cs] = xr
            xi_s[pl.ds(r0, SUBLANES), cs] = xi
            last = SUBLANES - 1
            return (jnp.broadcast_to(xr[last:, :], xr.shape), jnp.broadcast_to(xi[last:, :], xi.shape))

        cr, ci = lax.fori_loop(0, rows // SUBLANES, group, (cr_s[:, cs], ci_s[:, cs]))
        cr_s[:, cs] = cr
        ci_s[:, cs] = ci

    y = (jnp.dot(xr_s[...].astype(BF16), w2r_ref[...], preferred_element_type=F32)
         - jnp.dot(xi_s[...].astype(BF16), w2i_ref[...], preferred_element_type=F32))
    y = _gelu(y + d_ref[...] * u)
    z = jnp.dot(y.astype(BF16), wg_ref[...], preferred_element_type=F32) + bg_ref[...]
    o_ref[0] = y * jax.nn.sigmoid(z)


def _s5(proj, col_block, a_re, a_im, log_dt, b_re, b_im, c_re, c_im, d_skip, w_glu, b_glu, tb):
    bsz, seq, _ = proj.shape
    g, p = a_re.shape
    hh = b_re.shape[2]
    w = g * hh
    n = g * p
    pre, pim, bbre, bbim = _s5prep(a_re, a_im, log_dt, b_re, b_im)
    eye = jnp.eye(g, dtype=F32)

    def blockdiag(m):
        a, b = m.shape[1], m.shape[2]
        return (eye[:, None, :, None] * m[:, :, None, :]).reshape(g * a, g * b)

    w1 = jnp.concatenate([blockdiag(bbre), blockdiag(bbim)], axis=1).astype(BF16)
    w2r = blockdiag(jnp.swapaxes(c_re, 1, 2)).astype(BF16)
    w2i = blockdiag(jnp.swapaxes(c_im, 1, 2)).astype(BF16)
    pre = pre.reshape(SUBLANES, n)
    pim = pim.reshape(SUBLANES, n)
    rid = jnp.arange(SUBLANES)[:, None]

    def step_table(d):
        return (jnp.where(rid >= d, pre[d - 1][None, :], 0.0), jnp.where(rid >= d, pim[d - 1][None, :], 0.0))

    t1r, t1i = step_table(1)
    t2r, t2i = step_table(2)
    t4r, t4i = step_table(4)

    def full(a):
        return pl.BlockSpec(a.shape, lambda b, i: (0,) * a.ndim)

    consts = [w1, t1r, t1i, t2r, t2i, t4r, t4i, pre, pim, w2r, w2i,
              d_skip.reshape(1, w), w_glu.astype(BF16), b_glu.reshape(1, w)]
    return pl.pallas_call(
        _s5_kernel,
        grid=(bsz, seq // tb),
        in_specs=[pl.BlockSpec((1, tb, w), lambda b, i: (b, i, col_block))] + [full(a) for a in consts],
        out_specs=pl.BlockSpec((1, tb, w), lambda b, i: (b, i, 0)),
        out_shape=jax.ShapeDtypeStruct((bsz, seq, w), F32),
        scratch_shapes=[pltpu.VMEM((tb, n), F32), pltpu.VMEM((tb, n), F32),
                        pltpu.VMEM((SUBLANES, n), F32), pltpu.VMEM((SUBLANES, n), F32)],
        compiler_params=_cparams("arbitrary", "arbitrary", vmem=VMEM_LIMIT),
        name="s5",
    )(proj, *consts)


def _mixout_kernel(ohg_ref, os5_ref, x_ref, mod_ref, wo1_ref, wo2_ref, nw_ref, wq_ref, k1_ref, k2_ref,
                   x1_ref, h2_ref, sc_ref):
    mixed = (jnp.dot(ohg_ref[0].astype(BF16), wo1_ref[...], preferred_element_type=F32)
             + jnp.dot(os5_ref[0].astype(BF16), wo2_ref[...], preferred_element_type=F32))
    x1 = x_ref[0] + mod_ref[0, 2:3, :] * mixed
    x1_ref[0] = x1
    h2 = _norm_mod(x1, nw_ref[...], mod_ref[0, 4:5, :], mod_ref[0, 3:4, :])
    h2_ref[0] = h2
    q = jnp.dot(h2.astype(BF16), wq_ref[...], preferred_element_type=F32)
    for m in range(2 * PEER_HEADS):
        keys = k1_ref[...] if m % 2 == 0 else k2_ref[...]
        qh = q[:, m * PEER_HALF:(m + 1) * PEER_HALF].astype(BF16)
        sc_ref[0, m] = lax.dot_general(keys, qh, (((1,), (1,)), ((), ())), preferred_element_type=F32)


def _mixout(o_hg, o_s5, x, mod6, w_out, nw, w_q, keys1, keys2, tm):
    bsz, seq, d = x.shape
    w = o_hg.shape[2]
    nq = w_q.shape[1]
    wo = w_out.astype(BF16)

    def full(a):
        return pl.BlockSpec(a.shape, lambda b, i: (0,) * a.ndim)

    consts_a = [wo[:w], wo[w:]]
    consts_b = [nw.reshape(1, d), w_q.astype(BF16), keys1.astype(BF16), keys2.astype(BF16)]
    return pl.pallas_call(
        _mixout_kernel,
        grid=(bsz, seq // tm),
        in_specs=[pl.BlockSpec((1, tm, w), lambda b, i: (b, i, 0)),
                  pl.BlockSpec((1, tm, w), lambda b, i: (b, i, 0)),
                  pl.BlockSpec((1, tm, d), lambda b, i: (b, i, 0)),
                  pl.BlockSpec((1, 6, d), lambda b, i: (b, 0, 0))]
                 + [full(a) for a in consts_a] + [full(a) for a in consts_b],
        out_specs=[pl.BlockSpec((1, tm, d), lambda b, i: (b, i, 0)),
                   pl.BlockSpec((1, tm, d), lambda b, i: (b, i, 0)),
                   pl.BlockSpec((1, 2 * PEER_HEADS, PEER_NKEYS, tm), lambda b, i: (b, 0, 0, i))],
        out_shape=(jax.ShapeDtypeStruct((bsz, seq, d), F32),
                   jax.ShapeDtypeStruct((bsz, seq, d), F32),
                   jax.ShapeDtypeStruct((bsz, 2 * PEER_HEADS, PEER_NKEYS, seq), F32)),
        compiler_params=_cparams("arbitrary", "arbitrary", vmem=VMEM_LIMIT),
        name="mixout",
    )(o_hg, o_s5, x, mod6, *consts_a, *consts_b)


def _topk_rows(s, k):
    rows = s.shape[0]
    iota = lax.broadcasted_iota(I32, s.shape, 0)
    vals, idxs = [], []
    for _ in range(k):
        m = jnp.max(s, axis=0, keepdims=True)
        am = jnp.min(jnp.where(s == m, iota, rows), axis=0, keepdims=True)
        vals.append(m)
        idxs.append(am)
        s = jnp.where(iota == am, -jnp.inf, s)
    return vals, idxs


def _topk_kernel(sc_ref, e_ref, g_ref):
    k = PEER_TOPK
    tt = sc_ref.shape[3]

    def head(h, carry):
        v1, i1 = _topk_rows(sc_ref[0, 2 * h], k)
        v2, i2 = _topk_rows(sc_ref[0, 2 * h + 1], k)
        v2a = jnp.concatenate(v2, axis=0)
        i2a = jnp.concatenate(i2, axis=0)
        cand = jnp.concatenate([v1[a] + v2a for a in range(k)], axis=0)
        cidx = jnp.concatenate([i1[a] * PEER_NKEYS + i2a for a in range(k)], axis=0)
        iota = lax.broadcasted_iota(I32, cand.shape, 0)
        best, eidx = [], []
        for _ in range(k):
            m = jnp.max(cand, axis=0, keepdims=True)
            pos = jnp.min(jnp.where(cand == m, iota, k * k), axis=0, keepdims=True)
            hit = iota == pos
            eidx.append(jnp.max(jnp.where(hit, cidx, -1), axis=0, keepdims=True))
            best.append(m)
            cand = jnp.where(hit, -jnp.inf, cand)
        best = jnp.concatenate(best, axis=0)
        ex = jnp.exp(best - best[0:1, :])
        r0 = pl.multiple_of(h * k, k)
        g_ref[0, pl.ds(r0, k), :] = ex / jnp.sum(ex, axis=0, keepdims=True)
        e_ref[0, pl.ds(r0, k), :] = jnp.concatenate(eidx, axis=0)
        return carry

    lax.fori_loop(0, PEER_HEADS, head, 0)


def _topk(scores_t, tt):
    bsz, m, nk, seq = scores_t.shape
    rows = PEER_HEADS * PEER_TOPK
    return pl.pallas_call(
        _topk_kernel,
        grid=(bsz, seq // tt),
        in_specs=[pl.BlockSpec((1, m, nk, tt), lambda b, i: (b, 0, 0, i))],
        out_specs=[pl.BlockSpec((1, rows, tt), lambda b, i: (b, 0, i)),
                   pl.BlockSpec((1, rows, tt), lambda b, i: (b, 0, i))],
        out_shape=(jax.ShapeDtypeStruct((bsz, rows, seq), I32),
                   jax.ShapeDtypeStruct((bsz, rows, seq), F32)),
        compiler_params=_cparams("arbitrary", "arbitrary", vmem=VMEM_LIMIT),
        name="topk",
    )(scores_t)


def _pack_table(tab):
    e, d = tab.shape
    bits = lax.bitcast_convert_type(tab.astype(BF16), jnp.uint16).astype(jnp.uint32)
    bits = bits.reshape(e // 2, 2, d // LANES, LANES)
    word = bits[:, 0] | (bits[:, 1] << 16)
    return lax.bitcast_convert_type(word, I32)


def _expert_row(tab_ref, e):
    w = tab_ref[lax.shift_right_logical(e, 1)]
    sh = lax.shift_left(1 - (e & 1), 4)
    bits = lax.shift_left(w, sh) & jnp.int32(-65536)
    return pltpu.bitcast(bits, F32)


def _peer_u_kernel(idx_ref, h_ref, g_ref, tab_ref, o_ref):
    tb = h_ref.shape[0]
    nk = g_ref.shape[0]
    lane = lax.broadcasted_iota(I32, (nk, tb), 1)

    def token(t, acts):
        h = h_ref[t]
        rows = []
        for k in range(nk):
            p = _expert_row(tab_ref, idx_ref[k, t]) * h
            rows.append(jnp.sum(p, axis=0, keepdims=True))
        col = jnp.sum(jnp.concatenate(rows, axis=0), axis=1, keepdims=True)
        return jnp.where(lane == t, col, acts)

    acts = lax.fori_loop(0, tb, token, jnp.zeros((nk, tb), F32))
    o_ref[...] = g_ref[...] * _gelu(acts)


def _peer_u(eidx, h2r, gates, tab, tb):
    nk, nt = eidx.shape
    return pl.pallas_call(
        _peer_u_kernel,
        grid=(nt // tb,),
        in_specs=[pl.BlockSpec((nk, tb), lambda i: (0, i), memory_space=pltpu.SMEM),
                  pl.BlockSpec((tb, SUBLANES, LANES), lambda i: (i, 0, 0)),
                  pl.BlockSpec((nk, tb), lambda i: (0, i)),
                  pl.BlockSpec(tab.shape, lambda i: (0, 0, 0), pipeline_mode=pl.Buffered(1))],
        out_specs=pl.BlockSpec((nk, tb), lambda i: (0, i)),
        out_shape=jax.ShapeDtypeStruct((nk, nt), F32),
        compiler_params=_cparams("arbitrary", vmem=VMEM_LIMIT),
        name="peer_u",
    )(eidx, h2r, gates, tab)


def _peer_v_kernel(idx_ref, c_ref, tab_ref, o_ref):
    tb = o_ref.shape[0]
    nk = idx_ref.shape[0]
    n_acc = 4

    def token(t, carry):
        accs = [jnp.zeros((SUBLANES, LANES), F32) for _ in range(n_acc)]
        for k in range(nk):
            accs[k % n_acc] = accs[k % n_acc] + c_ref[k, t] * _expert_row(tab_ref, idx_ref[k, t])
        o_ref[t] = (accs[0] + accs[1]) + (accs[2] + accs[3])
        return carry

    lax.fori_loop(0, tb, token, 0)


def _peer_v(eidx, coef, tab, tb):
    nk, nt = eidx.shape
    return pl.pallas_call(
        _peer_v_kernel,
        grid=(nt // tb,),
        in_specs=[pl.BlockSpec((nk, tb), lambda i: (0, i), memory_space=pltpu.SMEM),
                  pl.BlockSpec((nk, tb), lambda i: (0, i), memory_space=pltpu.SMEM),
                  pl.BlockSpec(tab.shape, lambda i: (0, 0, 0), pipeline_mode=pl.Buffered(1))],
        out_specs=pl.BlockSpec((tb, SUBLANES, LANES), lambda i: (i, 0, 0)),
        out_shape=jax.ShapeDtypeStruct((nt, SUBLANES, LANES), F32),
        compiler_params=_cparams("arbitrary", vmem=VMEM_LIMIT),
        name="peer_v",
    )(eidx, coef, tab)


def _final_kernel(last, x_ref, p_ref, mod_ref, nw_ref, o_ref):
    x2 = x_ref[0] + mod_ref[0, 5:6, :] * p_ref[0]
    if last:
        ms = jnp.mean(x2 * x2, axis=-1, keepdims=True)
        x2 = x2 * lax.rsqrt(ms + EPS) * nw_ref[...]
    o_ref[0] = x2


def _final(x1, peer, mod6, nw, last, tm):
    bsz, seq, d = x1.shape
    blk = pl.BlockSpec((1, tm, d), lambda b, i: (b, i, 0))
    return pl.pallas_call(
        functools.partial(_final_kernel, last),
        grid=(bsz, seq // tm),
        in_specs=[blk, blk, pl.BlockSpec((1, 6, d), lambda b, i: (b, 0, 0)),
                  pl.BlockSpec((1, d), lambda b, i: (0, 0))],
        out_specs=blk,
        out_shape=jax.ShapeDtypeStruct((bsz, seq, d), F32),
        compiler_params=_cparams("arbitrary", "arbitrary", vmem=VMEM_LIMIT),
        name="final",
    )(x1, peer, mod6, nw.reshape(1, d))


def _tile(n, pref):
    t = min(n, pref)
    assert n % t == 0, (n, t)
    return t


def kernel(x, c, ada_w, ada_b, norm_mix_w, norm_ffn_w, w_in, w_out, hg_lower_bounds, hg_gnorm_w,
           s5_a_re, s5_a_im, s5_log_dt, s5_b_re, s5_b_im, s5_c_re, s5_c_im, s5_d, s5_glu_w, s5_glu_b,
           peer_wq, peer_keys1, peer_keys2, peer_u, peer_v, final_norm_w):
    bsz, seq, d = x.shape
    depth = ada_w.shape[0]
    nt = bsz * seq
    hg_w = HG_HEADS * HG_DK
    assert d == SUBLANES * LANES and seq % HG_CHUNK == 0
    assert w_in.shape[2] == 4 * hg_w + s5_a_re.shape[1] * S5_GROUP and 4 * hg_w % (s5_a_re.shape[1] * S5_GROUP) == 0
    s5_col = 4 * hg_w // (s5_a_re.shape[1] * S5_GROUP)

    for l in range(depth):
        mod6 = _mod(c, ada_w[l], ada_b[l]).reshape(bsz, 6, d)
        proj = _inproj(x, mod6, norm_mix_w[l], w_in[l].astype(BF16), _tile(seq, 512))
        o_hg = _hgrn2(proj, hg_lower_bounds, hg_gnorm_w[l], l, _tile(seq, 512))
        o_s5 = _s5(proj, s5_col, s5_a_re[l], s5_a_im[l], s5_log_dt[l], s5_b_re[l], s5_b_im[l],
                   s5_c_re[l], s5_c_im[l], s5_d[l], s5_glu_w[l], s5_glu_b[l], _tile(seq, 256))
        x1, h2, scores_t = _mixout(o_hg, o_s5, x, mod6, w_out[l], norm_ffn_w[l], peer_wq[l],
                                   peer_keys1[l], peer_keys2[l], _tile(seq, 256))
        eidx, gates = _topk(scores_t, _tile(seq, 256))
        rows = eidx.shape[1]
        eidx = jnp.swapaxes(eidx, 0, 1).reshape(rows, nt)
        gates = jnp.swapaxes(gates, 0, 1).reshape(rows, nt)
        tb = _tile(nt, 128)
        coef = _peer_u(eidx, h2.reshape(nt, SUBLANES, LANES), gates, _pack_table(peer_u[l]), tb)
        peer = _peer_v(eidx, coef, _pack_table(peer_v[l]), tb).reshape(bsz, seq, d)
        x = _final(x1, peer, mod6, final_norm_w, l == depth - 1, _tile(seq, 512))
    return x
```

```python
import functools
import math

import jax
import jax.numpy as jnp
from jax import lax
from jax.experimental import pallas as pl
from jax.experimental.pallas import tpu as pltpu

F32 = jnp.float32
BF16 = jnp.bfloat16
I32 = jnp.int32

EPS = 1e-6
HG_HEADS = 4
HG_DK = 128
HG_CHUNK = 64
HG_MAX_FACTORED_DECAY = 60.0
S5_GROUP = 16
S5_STATE = 64
PEER_HEADS = 8
PEER_NKEYS = 128
PEER_TOPK = 16
PEER_HALF = 128

LANES = 128
SUBLANES = 8
VMEM_LIMIT = 56 * 1024 * 1024

_HIGHEST = lax.Precision.HIGHEST
_SQRT_HALF = 0.7071067811865476


def _cparams(*sem, vmem=None):
    return pltpu.CompilerParams(dimension_semantics=sem, vmem_limit_bytes=vmem)


def _silu(x):
    return x * jax.nn.sigmoid(x)


def _gelu(x):
    return 0.5 * x * (1.0 + lax.erf(x * _SQRT_HALF))


def _bdot(a, b):
    return jnp.dot(a.astype(BF16), b.astype(BF16), preferred_element_type=F32)


def _mod_kernel(c_ref, w_ref, b_ref, o_ref):
    cond = _silu(c_ref[...])
    o_ref[...] = jnp.dot(cond, w_ref[...], preferred_element_type=F32, precision=_HIGHEST) + b_ref[...]


def _mod(c, w, b):
    bsz, d = c.shape
    n = w.shape[1]
    return pl.pallas_call(
        _mod_kernel,
        grid=(n // d,),
        in_specs=[pl.BlockSpec((bsz, d), lambda j: (0, 0)),
                  pl.BlockSpec((d, d), lambda j: (0, j)),
                  pl.BlockSpec((1, d), lambda j: (0, j))],
        out_specs=pl.BlockSpec((bsz, d), lambda j: (0, j)),
        out_shape=jax.ShapeDtypeStruct((bsz, n), F32),
        compiler_params=_cparams("arbitrary"),
        name="mod",
    )(c, w, b.reshape(1, n))


def _norm_mod(x, nw, scale, shift):
    ms = jnp.mean(x * x, axis=-1, keepdims=True)
    return (x * lax.rsqrt(ms + EPS) * nw) * (1.0 + scale) + shift


def _inproj_kernel(x_ref, mod_ref, nw_ref, w_ref, o_ref):
    h = _norm_mod(x_ref[0], nw_ref[...], mod_ref[0, 1:2, :], mod_ref[0, 0:1, :])
    o_ref[0] = jnp.dot(h.astype(BF16), w_ref[...], preferred_element_type=F32)


def _inproj(x, mod6, nw, w_bf16, tm):
    bsz, seq, d = x.shape
    n = w_bf16.shape[1]
    return pl.pallas_call(
        _inproj_kernel,
        grid=(bsz, seq // tm),
        in_specs=[pl.BlockSpec((1, tm, d), lambda b, i: (b, i, 0)),
                  pl.BlockSpec((1, 6, d), lambda b, i: (b, 0, 0)),
                  pl.BlockSpec((1, d), lambda b, i: (0, 0)),
                  pl.BlockSpec((d, n), lambda b, i: (0, 0))],
        out_specs=pl.BlockSpec((1, tm, n), lambda b, i: (b, i, 0)),
        out_shape=jax.ShapeDtypeStruct((bsz, seq, n), F32),
        compiler_params=_cparams("arbitrary", "arbitrary", vmem=VMEM_LIMIT),
        name="inproj",
    )(x, mod6, nw.reshape(1, d), w_bf16)


def _hgrn2_kernel(layer, q_ref, f_ref, i_ref, g_ref, lbp_ref, gnw_ref, o_ref,
                  st_ref, b_s, qs_s, oi_s):
    c_len = HG_CHUNK
    n_chunks = q_ref.shape[1] // c_len

    @pl.when(pl.program_id(1) == 0)
    def _():
        st_ref[...] = jnp.zeros_like(st_ref)

    lbp = lbp_ref[...]
    e = jnp.exp(lbp - jnp.max(lbp, axis=0, keepdims=True))
    lb_all = jnp.sum(e[: layer + 1], axis=0, keepdims=True) / jnp.sum(e, axis=0, keepdims=True)
    gnw = gnw_ref[...]
    row = lax.broadcasted_iota(I32, (c_len, c_len), 0)
    col = lax.broadcasted_iota(I32, (c_len, c_len), 1)
    tril = (col <= row).astype(F32)
    s_iota = lax.broadcasted_iota(I32, (c_len, 1), 0)

    def chunk(ci, carry):
        r0 = pl.multiple_of(ci * c_len, c_len)
        heads = []
        worst = None
        for h in range(HG_HEADS):
            cs = slice(h * HG_DK, (h + 1) * HG_DK)
            lb = lb_all[:, cs]
            f = lb + (1.0 - lb) * jax.nn.sigmoid(f_ref[0, pl.ds(r0, c_len), cs])
            kk = 1.0 - f
            qs = _silu(q_ref[0, pl.ds(r0, c_len), cs]) * (HG_DK ** -0.5)
            v = i_ref[0, pl.ds(r0, c_len), cs]
            b = jnp.dot(tril, jnp.log(f), preferred_element_type=F32, precision=_HIGHEST)
            heads.append((cs, kk, qs, v, b))
            decay = jnp.max(-b[c_len - 1:c_len, :])
            worst = decay if worst is None else jnp.maximum(worst, decay)

        def finish(h, o_intra):
            cs, kk, qs, v, b = heads[h]
            st = st_ref[h]
            o_inter = lax.dot_general((qs * jnp.exp(b)).astype(BF16), st.astype(BF16),
                                      (((1,), (1,)), ((), ())), preferred_element_type=F32)
            o = o_inter + o_intra
            b_last = b[c_len - 1:c_len, :]
            kdec = kk * jnp.exp(b_last - b)
            upd = lax.dot_general(v.astype(BF16), kdec.astype(BF16),
                                  (((0,), (0,)), ((), ())), preferred_element_type=F32)
            st_ref[h] = st * jnp.exp(b_last) + upd
            y = o * lax.rsqrt(jnp.mean(o * o, axis=-1, keepdims=True) + EPS) * gnw
            o_ref[0, pl.ds(r0, c_len), cs] = y * _silu(g_ref[0, pl.ds(r0, c_len), cs])

        factorable = worst < HG_MAX_FACTORED_DECAY

        @pl.when(factorable)
        def _():
            for h in range(HG_HEADS):
                cs, kk, qs, v, b = heads[h]
                qt = (qs * jnp.exp(b)).astype(BF16)
                kt = (kk * jnp.exp(-b)).astype(BF16)
                sc = lax.dot_general(qt, kt, (((1,), (1,)), ((), ())), preferred_element_type=F32)
                finish(h, jnp.dot((sc * tril).astype(BF16), v.astype(BF16), preferred_element_type=F32))

        @pl.when(jnp.logical_not(factorable))
        def _():
            for h in range(HG_HEADS):
                cs, kk, qs, v, b = heads[h]
                b_s[...] = b
                qs_s[...] = qs

                def row_step(t, c2, kk=kk, v=v, b=b):
                    bt = b_s[pl.ds(t, 1), :]
                    qt = qs_s[pl.ds(t, 1), :]
                    w = (qt * kk) * jnp.exp(jnp.minimum(bt - b, 0.0))
                    sc = jnp.sum(w, axis=1, keepdims=True)
                    sc = jnp.where(s_iota <= t, sc, 0.0)
                    oi_s[pl.ds(t, 1), :] = jnp.sum(sc * v, axis=0, keepdims=True)
                    return c2

                lax.fori_loop(0, c_len, row_step, 0)
                finish(h, oi_s[...])

        return carry

    lax.fori_loop(0, n_chunks, chunk, 0)


def _hgrn2(proj, lbp, gnw, layer, tb):
    bsz, seq, _ = proj.shape
    w = HG_HEADS * HG_DK
    nl = lbp.shape[0]

    def col_spec(j):
        return pl.BlockSpec((1, tb, w), lambda b, i, j=j: (b, i, j))

    return pl.pallas_call(
        functools.partial(_hgrn2_kernel, layer),
        grid=(bsz, seq // tb),
        in_specs=[col_spec(0), col_spec(1), col_spec(2), col_spec(3),
                  pl.BlockSpec((nl, w), lambda b, i: (0, 0)),
                  pl.BlockSpec((1, HG_DK), lambda b, i: (0, 0))],
        out_specs=pl.BlockSpec((1, tb, w), lambda b, i: (b, i, 0)),
        out_shape=jax.ShapeDtypeStruct((bsz, seq, w), F32),
        scratch_shapes=[pltpu.VMEM((HG_HEADS, HG_DK, HG_DK), F32),
                        pltpu.VMEM((HG_CHUNK, HG_DK), F32),
                        pltpu.VMEM((HG_CHUNK, HG_DK), F32),
                        pltpu.VMEM((HG_CHUNK, HG_DK), F32)],
        compiler_params=_cparams("arbitrary", "arbitrary", vmem=VMEM_LIMIT),
        name="hgrn2",
    )(proj, proj, proj, proj, lbp, gnw.reshape(1, HG_DK))


def _s5prep_kernel(are_ref, aim_ref, ldt_ref, bre_ref, bim_ref, pre_ref, pim_ref, bbre_ref, bbim_ref):
    ar = are_ref[...]
    ai = aim_ref[...]
    dt = jnp.exp(ldt_ref[...])
    mag = jnp.exp(ar * dt)
    abr = mag * jnp.cos(ai * dt)
    abi = mag * jnp.sin(ai * dt)
    den = ar * ar + ai * ai
    nr = abr - 1.0
    ni = abi
    cre = (nr * ar + ni * ai) / den
    cim = (ni * ar - nr * ai) / den
    bre = bre_ref[...]
    bim = bim_ref[...]
    bbre_ref[...] = cre[:, None, :] * bre - cim[:, None, :] * bim
    bbim_ref[...] = cre[:, None, :] * bim + cim[:, None, :] * bre
    pr, pi = abr, abi
    pre_ref[0] = pr
    pim_ref[0] = pi
    for k in range(1, SUBLANES):
        pr, pi = pr * abr - pi * abi, pr * abi + pi * abr
        pre_ref[k] = pr
        pim_ref[k] = pi


def _s5prep(a_re, a_im, log_dt, b_re, b_im):
    g, p = a_re.shape
    hh = b_re.shape[2]
    out = pl.pallas_call(
        _s5prep_kernel,
        out_shape=(jax.ShapeDtypeStruct((SUBLANES, g, p), F32), jax.ShapeDtypeStruct((SUBLANES, g, p), F32),
                   jax.ShapeDtypeStruct((g, hh, p), F32), jax.ShapeDtypeStruct((g, hh, p), F32)),
        name="s5prep",
    )(a_re, a_im, log_dt.reshape(g, 1), jnp.swapaxes(b_re, 1, 2), jnp.swapaxes(b_im, 1, 2))
    return out


S5_COLS = 256


def _s5_kernel(u_ref, w1_ref, t1r, t1i, t2r, t2i, t4r, t4i, pr_ref, pi_ref, w2r_ref, w2i_ref,
               d_ref, wg_ref, bg_ref, o_ref, xr_s, xi_s, cr_s, ci_s):
    n = xr_s.shape[1]
    rows = xr_s.shape[0]

    @pl.when(pl.program_id(1) == 0)
    def _():
        cr_s[...] = jnp.zeros_like(cr_s)
        ci_s[...] = jnp.zeros_like(ci_s)

    u = u_ref[0]
    bu = jnp.dot(u.astype(BF16), w1_ref[...], preferred_element_type=F32)
    xr_s[...] = bu[:, :n]
    xi_s[...] = bu[:, n:]

    for c0 in range(0, n, S5_COLS):
        cs = slice(c0, c0 + S5_COLS)
        steps = ((1, t1r[:, cs], t1i[:, cs]), (2, t2r[:, cs], t2i[:, cs]), (4, t4r[:, cs], t4i[:, cs]))
        pr = pr_ref[:, cs]
        pi = pi_ref[:, cs]

        def group(j, carry, cs=cs, steps=steps, pr=pr, pi=pi):
            cr, ci = carry
            r0 = pl.multiple_of(j * SUBLANES, SUBLANES)
            xr = xr_s[pl.ds(r0, SUBLANES), cs]
            xi = xi_s[pl.ds(r0, SUBLANES), cs]
            for d, ar, ai in steps:
                sr = pltpu.roll(xr, d, 0)
                si = pltpu.roll(xi, d, 0)
                xr, xi = xr + ar * sr - ai * si, xi + ar * si + ai * sr
            xr, xi = xr + pr * cr - pi * ci, xi + pr * ci + pi * cr
            xr_s[pl.ds(r0, SUBLANES), cs] = xr
            xi_s[pl.ds(r0, SUBLANES), cs] = xi
            last = SUBLANES - 1
            return (jnp.broadcast_to(xr[last:, :], xr.shape), jnp.broadcast_to(xi[last:, :], xi.shape))

        cr, ci = lax.fori_loop(0, rows // SUBLANES, group, (cr_s[:, cs], ci_s[:, cs]))
        cr_s[:, cs] = cr
        ci_s[:, cs] = ci

    y = (jnp.dot(xr_s[...].astype(BF16), w2r_ref[...], preferred_element_type=F32)
         - jnp.dot(xi_s[...].astype(BF16), w2i_ref[...], preferred_element_type=F32))
    y = _gelu(y + d_ref[...] * u)
    z = jnp.dot(y.astype(BF16), wg_ref[...], preferred_element_type=F32) + bg_ref[...]
    o_ref[0] = y * jax.nn.sigmoid(z)


def _s5(proj, col_block, a_re, a_im, log_dt, b_re, b_im, c_re, c_im, d_skip, w_glu, b_glu, tb):
    bsz, seq, _ = proj.shape
    g, p = a_re.shape
    hh = b_re.shape[2]
    w = g * hh
    n = g * p
    pre, pim, bbre, bbim = _s5prep(a_re, a_im, log_dt, b_re, b_im)
    eye = jnp.eye(g, dtype=F32)

    def blockdiag(m):
        a, b = m.shape[1], m.shape[2]
        return (eye[:, None, :, None] * m[:, :, None, :]).reshape(g * a, g * b)

    w1 = jnp.concatenate([blockdiag(bbre), blockdiag(bbim)], axis=1).astype(BF16)
    w2r = blockdiag(jnp.swapaxes(c_re, 1, 2)).astype(BF16)
    w2i = blockdiag(jnp.swapaxes(c_im, 1, 2)).astype(BF16)
    pre = pre.reshape(SUBLANES, n)
    pim = pim.reshape(SUBLANES, n)
    rid = jnp.arange(SUBLANES)[:, None]

    def step_table(d):
        return (jnp.where(rid >= d, pre[d - 1][None, :], 0.0), jnp.where(rid >= d, pim[d - 1][None, :], 0.0))

    t1r, t1i = step_table(1)
    t2r, t2i = step_table(2)
    t4r, t4i = step_table(4)

    def full(a):
        return pl.BlockSpec(a.shape, lambda b, i: (0,) * a.ndim)

    consts = [w1, t1r, t1i, t2r, t2i, t4r, t4i, pre, pim, w2r, w2i,
              d_skip.reshape(1, w), w_glu.astype(BF16), b_glu.reshape(1, w)]
    return pl.pallas_call(
        _s5_kernel,
        grid=(bsz, seq // tb),
        in_specs=[pl.BlockSpec((1, tb, w), lambda b, i: (b, i, col_block))] + [full(a) for a in consts],
        out_specs=pl.BlockSpec((1, tb, w), lambda b, i: (b, i, 0)),
        out_shape=jax.ShapeDtypeStruct((bsz, seq, w), F32),
        scratch_shapes=[pltpu.VMEM((tb, n), F32), pltpu.VMEM((tb, n), F32),
                        pltpu.VMEM((SUBLANES, n), F32), pltpu.VMEM((SUBLANES, n), F32)],
        compiler_params=_cparams("arbitrary", "arbitrary", vmem=VMEM_LIMIT),
        name="s5",
    )(proj, *consts)


def _mixout_kernel(ohg_ref, os5_ref, x_ref, mod_ref, wo1_ref, wo2_ref, nw_ref, wq_ref, k1_ref, k2_ref,
                   x1_ref, h2_ref, sc_ref):
    mixed = (jnp.dot(ohg_ref[0].astype(BF16), wo1_ref[...], preferred_element_type=F32)
             + jnp.dot(os5_ref[0].astype(BF16), wo2_ref[...], preferred_element_type=F32))
    x1 = x_ref[0] + mod_ref[0, 2:3, :] * mixed
    x1_ref[0] = x1
    h2 = _norm_mod(x1, nw_ref[...], mod_ref[0, 4:5, :], mod_ref[0, 3:4, :])
    h2_ref[0] = h2
    q = jnp.dot(h2.astype(BF16), wq_ref[...], preferred_element_type=F32)
    for m in range(2 * PEER_HEADS):
        keys = k1_ref[...] if m % 2 == 0 else k2_ref[...]
        qh = q[:, m * PEER_HALF:(m + 1) * PEER_HALF].astype(BF16)
        sc_ref[0, m] = lax.dot_general(keys, qh, (((1,), (1,)), ((), ())), preferred_element_type=F32)


def _mixout(o_hg, o_s5, x, mod6, w_out, nw, w_q, keys1, keys2, tm):
    bsz, seq, d = x.shape
    w = o_hg.shape[2]
    nq = w_q.shape[1]
    wo = w_out.astype(BF16)

    def full(a):
        return pl.BlockSpec(a.shape, lambda b, i: (0,) * a.ndim)

    consts_a = [wo[:w], wo[w:]]
    consts_b = [nw.reshape(1, d), w_q.astype(BF16), keys1.astype(BF16), keys2.astype(BF16)]
    return pl.pallas_call(
        _mixout_kernel,
        grid=(bsz, seq // tm),
        in_specs=[pl.BlockSpec((1, tm, w), lambda b, i: (b, i, 0)),
                  pl.BlockSpec((1, tm, w), lambda b, i: (b, i, 0)),
                  pl.BlockSpec((1, tm, d), lambda b, i: (b, i, 0)),
                  pl.BlockSpec((1, 6, d), lambda b, i: (b, 0, 0))]
                 + [full(a) for a in consts_a] + [full(a) for a in consts_b],
        out_specs=[pl.BlockSpec((1, tm, d), lambda b, i: (b, i, 0)),
                   pl.BlockSpec((1, tm, d), lambda b, i: (b, i, 0)),
                   pl.BlockSpec((1, 2 * PEER_HEADS, PEER_NKEYS, tm), lambda b, i: (b, 0, 0, i))],
        out_shape=(jax.ShapeDtypeStruct((bsz, seq, d), F32),
                   jax.ShapeDtypeStruct((bsz, seq, d), F32),
                   jax.ShapeDtypeStruct((bsz, 2 * PEER_HEADS, PEER_NKEYS, seq), F32)),
        compiler_params=_cparams("arbitrary", "arbitrary", vmem=VMEM_LIMIT),
        name="mixout",
    )(o_hg, o_s5, x, mod6, *consts_a, *consts_b)


def _topk_rows(s, k):
    rows = s.shape[0]
    iota = lax.broadcasted_iota(I32, s.shape, 0)
    vals, idxs = [], []
    for _ in range(k):
        m = jnp.max(s, axis=0, keepdims=True)
        am = jnp.min(jnp.where(s == m, iota, rows), axis=0, keepdims=True)
        vals.append(m)
        idxs.append(am)
        s = jnp.where(iota == am, -jnp.inf, s)
    return vals, idxs


def _topk_kernel(sc_ref, e_ref, g_ref):
    k = PEER_TOPK
    tt = sc_ref.shape[3]

    def head(h, carry):
        v1, i1 = _topk_rows(sc_ref[0, 2 * h], k)
        v2, i2 = _topk_rows(sc_ref[0, 2 * h + 1], k)
        v2a = jnp.concatenate(v2, axis=0)
        i2a = jnp.concatenate(i2, axis=0)
        cand = jnp.concatenate([v1[a] + v2a for a in range(k)], axis=0)
        cidx = jnp.concatenate([i1[a] * PEER_NKEYS + i2a for a in range(k)], axis=0)
        iota = lax.broadcasted_iota(I32, cand.shape, 0)
        best, eidx = [], []
        for _ in range(k):
            m = jnp.max(cand, axis=0, keepdims=True)
            pos = jnp.min(jnp.where(cand == m, iota, k * k), axis=0, keepdims=True)
            hit = iota == pos
            eidx.append(jnp.max(jnp.where(hit, cidx, -1), axis=0, keepdims=True))
            best.append(m)
            cand = jnp.where(hit, -jnp.inf, cand)
        best = jnp.concatenate(best, axis=0)
        ex = jnp.exp(best - best[0:1, :])
        r0 = pl.multiple_of(h * k, k)
        g_ref[0, pl.ds(r0, k), :] = ex / jnp.sum(ex, axis=0, keepdims=True)
        e_ref[0, pl.ds(r0, k), :] = jnp.concatenate(eidx, axis=0)
        return carry

    lax.fori_loop(0, PEER_HEADS, head, 0)


def _topk(scores_t, tt):
    bsz, m, nk, seq = scores_t.shape
    rows = PEER_HEADS * PEER_TOPK
    return pl.pallas_call(
        _topk_kernel,
        grid=(bsz, seq // tt),
        in_specs=[pl.BlockSpec((1, m, nk, tt), lambda b, i: (b, 0, 0, i))],
        out_specs=[pl.BlockSpec((1, rows, tt), lambda b, i: (b, 0, i)),
                   pl.BlockSpec((1, rows, tt), lambda b, i: (b, 0, i))],
        out_shape=(jax.ShapeDtypeStruct((bsz, rows, seq), I32),
                   jax.ShapeDtypeStruct((bsz, rows, seq), F32)),
        compiler_params=_cparams("arbitrary", "arbitrary", vmem=VMEM_LIMIT),
        name="topk",
    )(scores_t)


def _table_tiles(tab):
    e, d = tab.shape
    return tab.astype(BF16).reshape(e, d // LANES, LANES)


def _sublane_sums(p):
    row = lax.broadcasted_iota(I32, (SUBLANES, LANES), 0)
    m4 = row < 4
    m2 = (row & 3) < 2
    m1 = (row & 1) < 1

    def fold4(a, b):
        return jnp.where(m4, a, b) + pltpu.roll(jnp.where(m4, b, a), 4, 0)

    def fold(c1, c2, m, d):
        return (jnp.where(m, c1, pltpu.roll(c2, d, 0))
                + jnp.where(m, pltpu.roll(c1, SUBLANES - d, 0), c2))

    d1 = fold(fold4(p[0], p[4]), fold4(p[2], p[6]), m2, 2)
    d2 = fold(fold4(p[1], p[5]), fold4(p[3], p[7]), m2, 2)
    return fold(d1, d2, m1, 1)


def _peer_u_kernel(idx_ref, h_ref, g_ref, tab_ref, o_ref, acts_s, red_s):
    tb = h_ref.shape[0]
    nk = g_ref.shape[0]
    groups = nk // SUBLANES
    lane = lax.broadcasted_iota(I32, (SUBLANES, tb), 1)
    acts_s[...] = jnp.zeros_like(acts_s)
    red_s[...] = jnp.zeros_like(red_s)

    def finish(t):
        for g in range(groups):
            rows = slice(g * SUBLANES, (g + 1) * SUBLANES)
            col = jnp.sum(red_s[g], axis=1, keepdims=True)
            acts_s[rows, :] = jnp.where(lane == t, col, acts_s[rows, :])

    def token(t, carry):
        finish(t - 1)
        h = h_ref[t]
        base = t * nk
        for g in range(groups):
            p = [tab_ref[idx_ref[base + g * SUBLANES + i]].astype(F32) * h for i in range(SUBLANES)]
            red_s[g] = _sublane_sums(p)
        return carry

    lax.fori_loop(0, tb, token, 0)
    finish(tb - 1)
    o_ref[...] = g_ref[...] * _gelu(acts_s[...])


def _peer_u(eidx_flat, h2r, gates, tab, tb):
    nk, nt = gates.shape
    return pl.pallas_call(
        _peer_u_kernel,
        grid=(nt // tb,),
        in_specs=[pl.BlockSpec((tb * nk,), lambda i: (i,), memory_space=pltpu.SMEM),
                  pl.BlockSpec((tb, SUBLANES, LANES), lambda i: (i, 0, 0)),
                  pl.BlockSpec((nk, tb), lambda i: (0, i)),
                  pl.BlockSpec(tab.shape, lambda i: (0, 0, 0), pipeline_mode=pl.Buffered(1))],
        out_specs=pl.BlockSpec((nk, tb), lambda i: (0, i)),
        out_shape=jax.ShapeDtypeStruct((nk, nt), F32),
        scratch_shapes=[pltpu.VMEM((nk, tb), F32), pltpu.VMEM((nk // SUBLANES, SUBLANES, LANES), F32)],
        compiler_params=_cparams("arbitrary", vmem=VMEM_LIMIT),
        name="peer_u",
    )(eidx_flat, h2r, gates, tab)


def _peer_v_kernel(idx_ref, c_ref, tab_ref, o_ref, cb0_s, cb1_s):
    tb = o_ref.shape[0]
    nk = c_ref.shape[0]
    n_acc = 4
    lane = lax.broadcasted_iota(I32, (SUBLANES, tb), 1)

    def prep(t, cb):
        for g in range(nk // SUBLANES):
            rows = slice(g * SUBLANES, (g + 1) * SUBLANES)
            col = jnp.sum(jnp.where(lane == t, c_ref[rows, :], 0.0), axis=1, keepdims=True)
            cb[rows, :] = jnp.broadcast_to(col, (SUBLANES, LANES))

    def gather(t, cb):
        base = t * nk
        accs = [jnp.zeros((SUBLANES, LANES), F32) for _ in range(n_acc)]
        for k in range(nk):
            accs[k % n_acc] = accs[k % n_acc] + cb[k:k + 1, :] * tab_ref[idx_ref[base + k]].astype(F32)
        o_ref[t] = (accs[0] + accs[1]) + (accs[2] + accs[3])

    prep(0, cb0_s)

    def token(t, carry):
        prep(t + 1, cb1_s)
        gather(t, cb0_s)
        cb0_s[...] = cb1_s[...]
        return carry

    lax.fori_loop(0, tb, token, 0)


def _peer_v(eidx_flat, coef, tab, tb):
    nk, nt = coef.shape
    return pl.pallas_call(
        _peer_v_kernel,
        grid=(nt // tb,),
        in_specs=[pl.BlockSpec((tb * nk,), lambda i: (i,), memory_space=pltpu.SMEM),
                  pl.BlockSpec((nk, tb), lambda i: (0, i)),
                  pl.BlockSpec(tab.shape, lambda i: (0, 0, 0), pipeline_mode=pl.Buffered(1))],
        out_specs=pl.BlockSpec((tb, SUBLANES, LANES), lambda i: (i, 0, 0)),
        out_shape=jax.ShapeDtypeStruct((nt, SUBLANES, LANES), F32),
        scratch_shapes=[pltpu.VMEM((nk, LANES), F32), pltpu.VMEM((nk, LANES), F32)],
        compiler_params=_cparams("arbitrary", vmem=VMEM_LIMIT),
        name="peer_v",
    )(eidx_flat, coef, tab)


def _final_kernel(last, x_ref, p_ref, mod_ref, nw_ref, o_ref):
    x2 = x_ref[0] + mod_ref[0, 5:6, :] * p_ref[0]
    if last:
        ms = jnp.mean(x2 * x2, axis=-1, keepdims=True)
        x2 = x2 * lax.rsqrt(ms + EPS) * nw_ref[...]
    o_ref[0] = x2


def _final(x1, peer, mod6, nw, last, tm):
    bsz, seq, d = x1.shape
    blk = pl.BlockSpec((1, tm, d), lambda b, i: (b, i, 0))
    return pl.pallas_call(
        functools.partial(_final_kernel, last),
        grid=(bsz, seq // tm),
        in_specs=[blk, blk, pl.BlockSpec((1, 6, d), lambda b, i: (b, 0, 0)),
                  pl.BlockSpec((1, d), lambda b, i: (0, 0))],
        out_specs=blk,
        out_shape=jax.ShapeDtypeStruct((bsz, seq, d), F32),
        compiler_params=_cparams("arbitrary", "arbitrary", vmem=VMEM_LIMIT),
        name="final",
    )(x1, peer, mod6, nw.reshape(1, d))


def _tile(n, pref):
    t = min(n, pref)
    assert n % t == 0, (n, t)
    return t


def kernel(x, c, ada_w, ada_b, norm_mix_w, norm_ffn_w, w_in, w_out, hg_lower_bounds, hg_gnorm_w,
           s5_a_re, s5_a_im, s5_log_dt, s5_b_re, s5_b_im, s5_c_re, s5_c_im, s5_d, s5_glu_w, s5_glu_b,
           peer_wq, peer_keys1, peer_keys2, peer_u, peer_v, final_norm_w):
    bsz, seq, d = x.shape
    depth = ada_w.shape[0]
    nt = bsz * seq
    hg_w = HG_HEADS * HG_DK
    assert d == SUBLANES * LANES and seq % HG_CHUNK == 0
    assert w_in.shape[2] == 4 * hg_w + s5_a_re.shape[1] * S5_GROUP and 4 * hg_w % (s5_a_re.shape[1] * S5_GROUP) == 0
    s5_col = 4 * hg_w // (s5_a_re.shape[1] * S5_GROUP)

    for l in range(depth):
        mod6 = _mod(c, ada_w[l], ada_b[l]).reshape(bsz, 6, d)
        proj = _inproj(x, mod6, norm_mix_w[l], w_in[l].astype(BF16), _tile(seq, 512))
        o_hg = _hgrn2(proj, hg_lower_bounds, hg_gnorm_w[l], l, _tile(seq, 512))
        o_s5 = _s5(proj, s5_col, s5_a_re[l], s5_a_im[l], s5_log_dt[l], s5_b_re[l], s5_b_im[l],
                   s5_c_re[l], s5_c_im[l], s5_d[l], s5_glu_w[l], s5_glu_b[l], _tile(seq, 256))
        x1, h2, scores_t = _mixout(o_hg, o_s5, x, mod6, w_out[l], norm_ffn_w[l], peer_wq[l],
                                   peer_keys1[l], peer_keys2[l], _tile(seq, 256))
        eidx, gates = _topk(scores_t, _tile(seq, 256))
        rows = eidx.shape[1]
        eidx_flat = jnp.swapaxes(eidx, 1, 2).reshape(nt * rows)
        gates = jnp.swapaxes(gates, 0, 1).reshape(rows, nt)
        tb = _tile(nt, 128)
        coef = _peer_u(eidx_flat, h2.reshape(nt, SUBLANES, LANES), gates, _table_tiles(peer_u[l]), tb)
        peer = _peer_v(eidx_flat, coef, _table_tiles(peer_v[l]), tb).reshape(bsz, seq, d)
        x = _final(x1, peer, mod6, final_norm_w, l == depth - 1, _tile(seq, 512))
    return x
```

```python
import functools
import math

import jax
import jax.numpy as jnp
from jax import lax
from jax.experimental import pallas as pl
from jax.experimental.pallas import tpu as pltpu

F32 = jnp.float32
BF16 = jnp.bfloat16
I32 = jnp.int32

EPS = 1e-6
HG_HEADS = 4
HG_DK = 128
HG_CHUNK = 64
HG_MAX_FACTORED_DECAY = 60.0
S5_GROUP = 16
S5_STATE = 64
PEER_HEADS = 8
PEER_NKEYS = 128
PEER_TOPK = 16
PEER_HALF = 128

LANES = 128
SUBLANES = 8
VMEM_LIMIT = 56 * 1024 * 1024

_HIGHEST = lax.Precision.HIGHEST
_SQRT_HALF = 0.7071067811865476


def _cparams(*sem, vmem=None):
    return pltpu.CompilerParams(dimension_semantics=sem, vmem_limit_bytes=vmem)


def _silu(x):
    return x * jax.nn.sigmoid(x)


def _gelu(x):
    return 0.5 * x * (1.0 + lax.erf(x * _SQRT_HALF))


def _bdot(a, b):
    return jnp.dot(a.astype(BF16), b.astype(BF16), preferred_element_type=F32)


def _mod_kernel(c_ref, w_ref, b_ref, o_ref):
    cond = _silu(c_ref[...])
    o_ref[...] = jnp.dot(cond, w_ref[...], preferred_element_type=F32, precision=_HIGHEST) + b_ref[...]


def _mod(c, w, b):
    bsz, d = c.shape
    n = w.shape[1]
    return pl.pallas_call(
        _mod_kernel,
        grid=(n // d,),
        in_specs=[pl.BlockSpec((bsz, d), lambda j: (0, 0)),
                  pl.BlockSpec((d, d), lambda j: (0, j)),
                  pl.BlockSpec((1, d), lambda j: (0, j))],
        out_specs=pl.BlockSpec((bsz, d), lambda j: (0, j)),
        out_shape=jax.ShapeDtypeStruct((bsz, n), F32),
        compiler_params=_cparams("arbitrary"),
        name="mod",
    )(c, w, b.reshape(1, n))


def _norm_mod(x, nw, scale, shift):
    ms = jnp.mean(x * x, axis=-1, keepdims=True)
    return (x * lax.rsqrt(ms + EPS) * nw) * (1.0 + scale) + shift


def _inproj_kernel(x_ref, mod_ref, nw_ref, w_ref, o_ref):
    h = _norm_mod(x_ref[0], nw_ref[...], mod_ref[0, 1:2, :], mod_ref[0, 0:1, :])
    o_ref[0] = jnp.dot(h.astype(BF16), w_ref[...], preferred_element_type=F32)


def _inproj(x, mod6, nw, w_bf16, tm):
    bsz, seq, d = x.shape
    n = w_bf16.shape[1]
    return pl.pallas_call(
        _inproj_kernel,
        grid=(bsz, seq // tm),
        in_specs=[pl.BlockSpec((1, tm, d), lambda b, i: (b, i, 0)),
                  pl.BlockSpec((1, 6, d), lambda b, i: (b, 0, 0)),
                  pl.BlockSpec((1, d), lambda b, i: (0, 0)),
                  pl.BlockSpec((d, n), lambda b, i: (0, 0))],
        out_specs=pl.BlockSpec((1, tm, n), lambda b, i: (b, i, 0)),
        out_shape=jax.ShapeDtypeStruct((bsz, seq, n), F32),
        compiler_params=_cparams("arbitrary", "arbitrary", vmem=VMEM_LIMIT),
        name="inproj",
    )(x, mod6, nw.reshape(1, d), w_bf16)


def _hgrn2_kernel(layer, q_ref, f_ref, i_ref, g_ref, lbp_ref, gnw_ref, o_ref,
                  st_ref, b_s, qs_s, oi_s):
    c_len = HG_CHUNK
    n_chunks = q_ref.shape[1] // c_len

    @pl.when(pl.program_id(1) == 0)
    def _():
        st_ref[...] = jnp.zeros_like(st_ref)

    lbp = lbp_ref[...]
    e = jnp.exp(lbp - jnp.max(lbp, axis=0, keepdims=True))
    lb_all = jnp.sum(e[: layer + 1], axis=0, keepdims=True) / jnp.sum(e, axis=0, keepdims=True)
    gnw = gnw_ref[...]
    row = lax.broadcasted_iota(I32, (c_len, c_len), 0)
    col = lax.broadcasted_iota(I32, (c_len, c_len), 1)
    tril = (col <= row).astype(F32)
    s_iota = lax.broadcasted_iota(I32, (c_len, 1), 0)

    def chunk(ci, carry):
        r0 = pl.multiple_of(ci * c_len, c_len)
        heads = []
        worst = None
        for h in range(HG_HEADS):
            cs = slice(h * HG_DK, (h + 1) * HG_DK)
            lb = lb_all[:, cs]
            f = lb + (1.0 - lb) * jax.nn.sigmoid(f_ref[0, pl.ds(r0, c_len), cs])
            kk = 1.0 - f
            qs = _silu(q_ref[0, pl.ds(r0, c_len), cs]) * (HG_DK ** -0.5)
            v = i_ref[0, pl.ds(r0, c_len), cs]
            b = jnp.dot(tril, jnp.log(f), preferred_element_type=F32, precision=_HIGHEST)
            heads.append((cs, kk, qs, v, b))
            decay = jnp.max(-b[c_len - 1:c_len, :])
            worst = decay if worst is None else jnp.maximum(worst, decay)

        def finish(h, o_intra):
            cs, kk, qs, v, b = heads[h]
            st = st_ref[h]
            o_inter = lax.dot_general((qs * jnp.exp(b)).astype(BF16), st.astype(BF16),
                                      (((1,), (1,)), ((), ())), preferred_element_type=F32)
            o = o_inter + o_intra
            b_last = b[c_len - 1:c_len, :]
            kdec = kk * jnp.exp(b_last - b)
            upd = lax.dot_general(v.astype(BF16), kdec.astype(BF16),
                                  (((0,), (0,)), ((), ())), preferred_element_type=F32)
            st_ref[h] = st * jnp.exp(b_last) + upd
            y = o * lax.rsqrt(jnp.mean(o * o, axis=-1, keepdims=True) + EPS) * gnw
            o_ref[0, pl.ds(r0, c_len), cs] = y * _silu(g_ref[0, pl.ds(r0, c_len), cs])

        factorable = worst < HG_MAX_FACTORED_DECAY

        @pl.when(factorable)
        def _():
            for h in range(HG_HEADS):
                cs, kk, qs, v, b = heads[h]
                qt = (qs * jnp.exp(b)).astype(BF16)
                kt = (kk * jnp.exp(-b)).astype(BF16)
                sc = lax.dot_general(qt, kt, (((1,), (1,)), ((), ())), preferred_element_type=F32)
                finish(h, jnp.dot((sc * tril).astype(BF16), v.astype(BF16), preferred_element_type=F32))

        @pl.when(jnp.logical_not(factorable))
        def _():
            for h in range(HG_HEADS):
                cs, kk, qs, v, b = heads[h]
                b_s[...] = b
                qs_s[...] = qs

                def row_step(t, c2, kk=kk, v=v, b=b):
                    bt = b_s[pl.ds(t, 1), :]
                    qt = qs_s[pl.ds(t, 1), :]
                    w = (qt * kk) * jnp.exp(jnp.minimum(bt - b, 0.0))
                    sc = jnp.sum(w, axis=1, keepdims=True)
                    sc = jnp.where(s_iota <= t, sc, 0.0)
                    oi_s[pl.ds(t, 1), :] = jnp.sum(sc * v, axis=0, keepdims=True)
                    return c2

                lax.fori_loop(0, c_len, row_step, 0)
                finish(h, oi_s[...])

        return carry

    lax.fori_loop(0, n_chunks, chunk, 0)


def _hgrn2(proj, lbp, gnw, layer, tb):
    bsz, seq, _ = proj.shape
    w = HG_HEADS * HG_DK
    nl = lbp.shape[0]

    def col_spec(j):
        return pl.BlockSpec((1, tb, w), lambda b, i, j=j: (b, i, j))

    return pl.pallas_call(
        functools.partial(_hgrn2_kernel, layer),
        grid=(bsz, seq // tb),
        in_specs=[col_spec(0), col_spec(1), col_spec(2), col_spec(3),
                  pl.BlockSpec((nl, w), lambda b, i: (0, 0)),
                  pl.BlockSpec((1, HG_DK), lambda b, i: (0, 0))],
        out_specs=pl.BlockSpec((1, tb, w), lambda b, i: (b, i, 0)),
        out_shape=jax.ShapeDtypeStruct((bsz, seq, w), F32),
        scratch_shapes=[pltpu.VMEM((HG_HEADS, HG_DK, HG_DK), F32),
                        pltpu.VMEM((HG_CHUNK, HG_DK), F32),
                        pltpu.VMEM((HG_CHUNK, HG_DK), F32),
                        pltpu.VMEM((HG_CHUNK, HG_DK), F32)],
        compiler_params=_cparams("arbitrary", "arbitrary", vmem=VMEM_LIMIT),
        name="hgrn2",
    )(proj, proj, proj, proj, lbp, gnw.reshape(1, HG_DK))


def _s5prep_kernel(are_ref, aim_ref, ldt_ref, bre_ref, bim_ref, pre_ref, pim_ref, bbre_ref, bbim_ref):
    ar = are_ref[...]
    ai = aim_ref[...]
    dt = jnp.exp(ldt_ref[...])
    mag = jnp.exp(ar * dt)
    abr = mag * jnp.cos(ai * dt)
    abi = mag * jnp.sin(ai * dt)
    den = ar * ar + ai * ai
    nr = abr - 1.0
    ni = abi
    cre = (nr * ar + ni * ai) / den
    cim = (ni * ar - nr * ai) / den
    bre = bre_ref[...]
    bim = bim_ref[...]
    bbre_ref[...] = cre[:, None, :] * bre - cim[:, None, :] * bim
    bbim_ref[...] = cre[:, None, :] * bim + cim[:, None, :] * bre
    pr, pi = abr, abi
    pre_ref[0] = pr
    pim_ref[0] = pi
    for k in range(1, SUBLANES):
        pr, pi = pr * abr - pi * abi, pr * abi + pi * abr
        pre_ref[k] = pr
        pim_ref[k] = pi


def _s5prep(a_re, a_im, log_dt, b_re, b_im):
    g, p = a_re.shape
    hh = b_re.shape[2]
    out = pl.pallas_call(
        _s5prep_kernel,
        out_shape=(jax.ShapeDtypeStruct((SUBLANES, g, p), F32), jax.ShapeDtypeStruct((SUBLANES, g, p), F32),
                   jax.ShapeDtypeStruct((g, hh, p), F32), jax.ShapeDtypeStruct((g, hh, p), F32)),
        name="s5prep",
    )(a_re, a_im, log_dt.reshape(g, 1), jnp.swapaxes(b_re, 1, 2), jnp.swapaxes(b_im, 1, 2))
    return out


S5_COLS = 256
S5_UNROLL = 4
S5_DIAG = 2


def _s5_kernel(u_ref, w1_ref, t1r, t1i, t2r, t2i, t4r, t4i, pr_ref, pi_ref, w2r_ref, w2i_ref,
               d_ref, wg_ref, bg_ref, o_ref, xr_s, xi_s, cr_s, ci_s):
    n = xr_s.shape[1]
    rows = xr_s.shape[0]

    @pl.when(pl.program_id(1) == 0)
    def _():
        cr_s[...] = jnp.zeros_like(cr_s)
        ci_s[...] = jnp.zeros_like(ci_s)

    u = u_ref[0]
    ub = u.astype(BF16)
    w = u.shape[1]
    wq, nq = w // S5_DIAG, n // S5_DIAG
    for q in range(S5_DIAG):
        bu = jnp.dot(ub[:, q * wq:(q + 1) * wq], w1_ref[q], preferred_element_type=F32)
        xr_s[:, q * nq:(q + 1) * nq] = bu[:, :nq]
        xi_s[:, q * nq:(q + 1) * nq] = bu[:, nq:]

    for c0 in range(0, n, S5_COLS):
        cs = slice(c0, c0 + S5_COLS)
        steps = ((1, t1r[:, cs], t1i[:, cs]), (2, t2r[:, cs], t2i[:, cs]), (4, t4r[:, cs], t4i[:, cs]))
        pr = pr_ref[:, cs]
        pi = pi_ref[:, cs]

        def block(j, carry, cs=cs, steps=steps, pr=pr, pi=pi):
            cr, ci = carry
            local = []
            for q in range(S5_UNROLL):
                r0 = pl.multiple_of((j * S5_UNROLL + q) * SUBLANES, SUBLANES)
                xr = xr_s[pl.ds(r0, SUBLANES), cs]
                xi = xi_s[pl.ds(r0, SUBLANES), cs]
                for d, ar, ai in steps:
                    sr = pltpu.roll(xr, d, 0)
                    si = pltpu.roll(xi, d, 0)
                    xr, xi = xr + ar * sr - ai * si, xi + ar * si + ai * sr
                local.append((r0, xr, xi))
            last = SUBLANES - 1
            for r0, xr, xi in local:
                xr, xi = xr + pr * cr - pi * ci, xi + pr * ci + pi * cr
                xr_s[pl.ds(r0, SUBLANES), cs] = xr
                xi_s[pl.ds(r0, SUBLANES), cs] = xi
                cr = jnp.broadcast_to(xr[last:, :], xr.shape)
                ci = jnp.broadcast_to(xi[last:, :], xi.shape)
            return cr, ci

        cr, ci = lax.fori_loop(0, rows // (SUBLANES * S5_UNROLL), block, (cr_s[:, cs], ci_s[:, cs]))
        cr_s[:, cs] = cr
        ci_s[:, cs] = ci

    ys = []
    for q in range(S5_DIAG):
        ns = slice(q * nq, (q + 1) * nq)
        ys.append(jnp.dot(xr_s[:, ns].astype(BF16), w2r_ref[q], preferred_element_type=F32)
                  - jnp.dot(xi_s[:, ns].astype(BF16), w2i_ref[q], preferred_element_type=F32))
    y = jnp.concatenate(ys, axis=1)
    y = _gelu(y + d_ref[...] * u)
    z = jnp.dot(y.astype(BF16), wg_ref[...], preferred_element_type=F32) + bg_ref[...]
    o_ref[0] = y * jax.nn.sigmoid(z)


def _s5(proj, col_block, a_re, a_im, log_dt, b_re, b_im, c_re, c_im, d_skip, w_glu, b_glu, tb):
    bsz, seq, _ = proj.shape
    g, p = a_re.shape
    hh = b_re.shape[2]
    w = g * hh
    n = g * p
    pre, pim, bbre, bbim = _s5prep(a_re, a_im, log_dt, b_re, b_im)
    gq = g // S5_DIAG
    eye = jnp.eye(gq, dtype=F32)

    def blockdiag(m):
        a, b = m.shape[1], m.shape[2]
        m = m.reshape(S5_DIAG, gq, a, b)
        return (eye[None, :, None, :, None] * m[:, :, :, None, :]).reshape(S5_DIAG, gq * a, gq * b)

    w1 = jnp.concatenate([blockdiag(bbre), blockdiag(bbim)], axis=2).astype(BF16)
    w2r = blockdiag(jnp.swapaxes(c_re, 1, 2)).astype(BF16)
    w2i = blockdiag(jnp.swapaxes(c_im, 1, 2)).astype(BF16)
    pre = pre.reshape(SUBLANES, n)
    pim = pim.reshape(SUBLANES, n)
    rid = jnp.arange(SUBLANES)[:, None]

    def step_table(d):
        return (jnp.where(rid >= d, pre[d - 1][None, :], 0.0), jnp.where(rid >= d, pim[d - 1][None, :], 0.0))

    t1r, t1i = step_table(1)
    t2r, t2i = step_table(2)
    t4r, t4i = step_table(4)

    def full(a):
        return pl.BlockSpec(a.shape, lambda b, i: (0,) * a.ndim)

    consts = [w1, t1r, t1i, t2r, t2i, t4r, t4i, pre, pim, w2r, w2i,
              d_skip.reshape(1, w), w_glu.astype(BF16), b_glu.reshape(1, w)]
    return pl.pallas_call(
        _s5_kernel,
        grid=(bsz, seq // tb),
        in_specs=[pl.BlockSpec((1, tb, w), lambda b, i: (b, i, col_block))] + [full(a) for a in consts],
        out_specs=pl.BlockSpec((1, tb, w), lambda b, i: (b, i, 0)),
        out_shape=jax.ShapeDtypeStruct((bsz, seq, w), F32),
        scratch_shapes=[pltpu.VMEM((tb, n), F32), pltpu.VMEM((tb, n), F32),
                        pltpu.VMEM((SUBLANES, n), F32), pltpu.VMEM((SUBLANES, n), F32)],
        compiler_params=_cparams("arbitrary", "arbitrary", vmem=VMEM_LIMIT),
        name="s5",
    )(proj, *consts)


def _mixout_kernel(ohg_ref, os5_ref, x_ref, mod_ref, wo1_ref, wo2_ref, nw_ref, wq_ref, k1_ref, k2_ref,
                   x1_ref, h2_ref, sc_ref):
    mixed = (jnp.dot(ohg_ref[0].astype(BF16), wo1_ref[...], preferred_element_type=F32)
             + jnp.dot(os5_ref[0].astype(BF16), wo2_ref[...], preferred_element_type=F32))
    x1 = x_ref[0] + mod_ref[0, 2:3, :] * mixed
    x1_ref[0] = x1
    h2 = _norm_mod(x1, nw_ref[...], mod_ref[0, 4:5, :], mod_ref[0, 3:4, :])
    h2_ref[0] = h2
    q = jnp.dot(h2.astype(BF16), wq_ref[...], preferred_element_type=F32)
    for m in range(2 * PEER_HEADS):
        keys = k1_ref[...] if m % 2 == 0 else k2_ref[...]
        qh = q[:, m * PEER_HALF:(m + 1) * PEER_HALF].astype(BF16)
        sc_ref[0, m] = lax.dot_general(keys, qh, (((1,), (1,)), ((), ())), preferred_element_type=F32)


def _mixout(o_hg, o_s5, x, mod6, w_out, nw, w_q, keys1, keys2, tm):
    bsz, seq, d = x.shape
    w = o_hg.shape[2]
    nq = w_q.shape[1]
    wo = w_out.astype(BF16)

    def full(a):
        return pl.BlockSpec(a.shape, lambda b, i: (0,) * a.ndim)

    consts_a = [wo[:w], wo[w:]]
    consts_b = [nw.reshape(1, d), w_q.astype(BF16), keys1.astype(BF16), keys2.astype(BF16)]
    return pl.pallas_call(
        _mixout_kernel,
        grid=(bsz, seq // tm),
        in_specs=[pl.BlockSpec((1, tm, w), lambda b, i: (b, i, 0)),
                  pl.BlockSpec((1, tm, w), lambda b, i: (b, i, 0)),
                  pl.BlockSpec((1, tm, d), lambda b, i: (b, i, 0)),
                  pl.BlockSpec((1, 6, d), lambda b, i: (b, 0, 0))]
                 + [full(a) for a in consts_a] + [full(a) for a in consts_b],
        out_specs=[pl.BlockSpec((1, tm, d), lambda b, i: (b, i, 0)),
                   pl.BlockSpec((1, tm, d), lambda b, i: (b, i, 0)),
                   pl.BlockSpec((1, 2 * PEER_HEADS, PEER_NKEYS, tm), lambda b, i: (b, 0, 0, i))],
        out_shape=(jax.ShapeDtypeStruct((bsz, seq, d), F32),
                   jax.ShapeDtypeStruct((bsz, seq, d), F32),
                   jax.ShapeDtypeStruct((bsz, 2 * PEER_HEADS, PEER_NKEYS, seq), F32)),
        compiler_params=_cparams("arbitrary", "arbitrary", vmem=VMEM_LIMIT),
        name="mixout",
    )(o_hg, o_s5, x, mod6, *consts_a, *consts_b)


def _topk_rows(s, k):
    rows = s.shape[0]
    iota = lax.broadcasted_iota(I32, s.shape, 0)
    vals, idxs = [], []
    for _ in range(k):
        m = jnp.max(s, axis=0, keepdims=True)
        am = jnp.min(jnp.where(s == m, iota, rows), axis=0, keepdims=True)
        vals.append(m)
        idxs.append(am)
        s = jnp.where(iota == am, -jnp.inf, s)
    return vals, idxs


def _topk_kernel(sc_ref, e_ref, g_ref):
    k = PEER_TOPK
    tt = sc_ref.shape[3]
    assert k == 2 * SUBLANES

    def head(h, carry):
        v1, i1 = _topk_rows(sc_ref[0, 2 * h], k)
        v2, i2 = _topk_rows(sc_ref[0, 2 * h + 1], k)
        v1a = jnp.concatenate(v1, axis=0)
        i1a = jnp.concatenate(i1, axis=0)
        v2a = jnp.concatenate(v2, axis=0)
        i2a = jnp.concatenate(i2, axis=0)
        row = lax.broadcasted_iota(I32, (SUBLANES, tt), 0)
        cands, cidxs, cposs = [], [], []
        tail = k // 2
        for a in range(tail):
            nb = k // (a + 1)
            for b0 in range(0, nb, SUBLANES):
                c = v1[a] + v2a[b0:b0 + SUBLANES]
                if nb - b0 < SUBLANES:
                    c = jnp.where(row < nb - b0, c, -jnp.inf)
                cands.append(c)
                cidxs.append(i1[a] * PEER_NKEYS + i2a[b0:b0 + SUBLANES])
                cposs.append(a * k + b0 + row)
        cands.append(v1a[tail:] + v2[0])
        cidxs.append(i1a[tail:] * PEER_NKEYS + i2[0])
        cposs.append((tail + row) * k)

        def over_tiles(op, tiles):
            return functools.reduce(op, tiles)

        best, eidx = [], []
        for _ in range(k):
            m = jnp.max(over_tiles(jnp.maximum, cands), axis=0, keepdims=True)
            pos = jnp.min(over_tiles(jnp.minimum, [jnp.where(c == m, cp, k * k) for c, cp in zip(cands, cposs)]),
                          axis=0, keepdims=True)
            hits = [cp == pos for cp in cposs]
            eidx.append(jnp.max(over_tiles(jnp.maximum, [jnp.where(hh, ci, -1) for hh, ci in zip(hits, cidxs)]),
                                axis=0, keepdims=True))
            best.append(m)
            cands = [jnp.where(hh, -jnp.inf, c) for hh, c in zip(hits, cands)]
        best = jnp.concatenate(best, axis=0)
        ex = jnp.exp(best - best[0:1, :])
        r0 = pl.multiple_of(h * k, k)
        g_ref[0, pl.ds(r0, k), :] = ex / jnp.sum(ex, axis=0, keepdims=True)
        e_ref[0, pl.ds(r0, k), :] = jnp.concatenate(eidx, axis=0)
        return carry

    lax.fori_loop(0, PEER_HEADS, head, 0)


def _topk(scores_t, tt):
    bsz, m, nk, seq = scores_t.shape
    rows = PEER_HEADS * PEER_TOPK
    return pl.pallas_call(
        _topk_kernel,
        grid=(bsz, seq // tt),
        in_specs=[pl.BlockSpec((1, m, nk, tt), lambda b, i: (b, 0, 0, i))],
        out_specs=[pl.BlockSpec((1, rows, tt), lambda b, i: (b, 0, i)),
                   pl.BlockSpec((1, rows, tt), lambda b, i: (b, 0, i))],
        out_shape=(jax.ShapeDtypeStruct((bsz, rows, seq), I32),
                   jax.ShapeDtypeStruct((bsz, rows, seq), F32)),
        compiler_params=_cparams("arbitrary", "arbitrary", vmem=VMEM_LIMIT),
        name="topk",
    )(scores_t)


def _table_tiles(tab):
    e, d = tab.shape
    return tab.astype(BF16).reshape(e, d // LANES, LANES)


def _sublane_sums(p):
    row = lax.broadcasted_iota(I32, (SUBLANES, LANES), 0)
    m4 = row < 4
    m2 = (row & 3) < 2
    m1 = (row & 1) < 1

    def fold4(a, b):
        return jnp.where(m4, a, b) + pltpu.roll(jnp.where(m4, b, a), 4, 0)

    def fold(c1, c2, m, d):
        return (jnp.where(m, c1, pltpu.roll(c2, d, 0))
                + jnp.where(m, pltpu.roll(c1, SUBLANES - d, 0), c2))

    d1 = fold(fold4(p[0], p[4]), fold4(p[2], p[6]), m2, 2)
    d2 = fold(fold4(p[1], p[5]), fold4(p[3], p[7]), m2, 2)
    return fold(d1, d2, m1, 1)


def _stage_ids(idx_hbm, id_refs, sems, tb):
    i = pl.program_id(0)

    def copies(blk, buf):
        return [pltpu.make_async_copy(idx_hbm.at[k, pl.ds(blk * tb, tb)], ref.at[pl.ds(buf * tb, tb)], sems.at[buf])
                for k, ref in enumerate(id_refs)]

    @pl.when(i == 0)
    def _():
        for cp in copies(0, 0):
            cp.start()

    @pl.when(i + 1 < pl.num_programs(0))
    def _():
        for cp in copies(i + 1, (i + 1) % 2):
            cp.start()

    for cp in copies(i, i % 2):
        cp.wait()
    return (i % 2) * tb


def _peer_u_kernel(idx_hbm, h_ref, g_ref, tab_ref, o_ref, acts_s, red_s, sems, *id_refs):
    nk, tb = g_ref.shape
    groups = nk // SUBLANES
    lane = lax.broadcasted_iota(I32, (SUBLANES, tb), 1)
    id0 = _stage_ids(idx_hbm, id_refs, sems, tb)
    acts_s[...] = jnp.zeros_like(acts_s)
    red_s[...] = jnp.zeros_like(red_s)

    def finish(t):
        for g in range(groups):
            rows = slice(g * SUBLANES, (g + 1) * SUBLANES)
            col = jnp.sum(red_s[g], axis=1, keepdims=True)
            acts_s[rows, :] = jnp.where(lane == t, col, acts_s[rows, :])

    def token(t, carry):
        finish(t - 1)
        h = h_ref[t]
        ti = id0 + t
        for g in range(groups):
            p = [tab_ref[id_refs[g * SUBLANES + i][ti]].astype(F32) * h for i in range(SUBLANES)]
            red_s[g] = _sublane_sums(p)
        return carry

    lax.fori_loop(0, tb, token, 0)
    finish(tb - 1)
    o_ref[...] = g_ref[...] * _gelu(acts_s[...])


def _peer_u(eidx, h2r, gates, tab, tb):
    nk, nt = gates.shape
    return pl.pallas_call(
        _peer_u_kernel,
        grid=(nt // tb,),
        in_specs=[pl.BlockSpec(memory_space=pl.ANY),
                  pl.BlockSpec((tb, SUBLANES, LANES), lambda i: (i, 0, 0)),
                  pl.BlockSpec((nk, tb), lambda i: (0, i)),
                  pl.BlockSpec(tab.shape, lambda i: (0, 0, 0), pipeline_mode=pl.Buffered(1))],
        out_specs=pl.BlockSpec((nk, tb), lambda i: (0, i)),
        out_shape=jax.ShapeDtypeStruct((nk, nt), F32),
        scratch_shapes=[pltpu.VMEM((nk, tb), F32), pltpu.VMEM((nk // SUBLANES, SUBLANES, LANES), F32),
                        pltpu.SemaphoreType.DMA((2,))] + [pltpu.SMEM((2 * tb,), I32)] * nk,
        compiler_params=_cparams("arbitrary", vmem=VMEM_LIMIT),
        name="peer_u",
    )(eidx, h2r, gates, tab)


def _peer_v_kernel(idx_hbm, c_ref, tab_ref, o_ref, cb0_s, cb1_s, part_s, sems, *id_refs):
    nk, tb = c_ref.shape
    groups = nk // SUBLANES
    lane = lax.broadcasted_iota(I32, (SUBLANES, tb), 1)
    id0 = _stage_ids(idx_hbm, id_refs, sems, tb)

    def prep(t, cb):
        for g in range(groups):
            rows = slice(g * SUBLANES, (g + 1) * SUBLANES)
            col = jnp.sum(jnp.where(lane == t, c_ref[rows, :], 0.0), axis=1, keepdims=True)
            cb[rows, :] = jnp.broadcast_to(col, (SUBLANES, LANES))

    def gather(t, cb):
        ti = id0 + t
        for g in range(groups):
            p = [cb[k:k + 1, :] * tab_ref[id_refs[k][ti]].astype(F32)
                 for k in range(g * SUBLANES, (g + 1) * SUBLANES)]
            part_s[g] = ((p[0] + p[1]) + (p[2] + p[3])) + ((p[4] + p[5]) + (p[6] + p[7]))

    def finish(t):
        s = [part_s[g] for g in range(groups)]
        while len(s) > 1:
            s = [s[i] + s[i + 1] for i in range(0, len(s), 2)]
        o_ref[t] = s[0]

    prep(0, cb0_s)
    part_s[...] = jnp.zeros_like(part_s)

    def token(t, carry):
        prep(t + 1, cb1_s)
        finish(jnp.maximum(t - 1, 0))
        gather(t, cb0_s)
        cb0_s[...] = cb1_s[...]
        return carry

    lax.fori_loop(0, tb, token, 0)
    finish(tb - 1)


def _peer_v(eidx, coef, tab, tb):
    nk, nt = coef.shape
    return pl.pallas_call(
        _peer_v_kernel,
        grid=(nt // tb,),
        in_specs=[pl.BlockSpec(memory_space=pl.ANY),
                  pl.BlockSpec((nk, tb), lambda i: (0, i)),
                  pl.BlockSpec(tab.shape, lambda i: (0, 0, 0), pipeline_mode=pl.Buffered(1))],
        out_specs=pl.BlockSpec((tb, SUBLANES, LANES), lambda i: (i, 0, 0)),
        out_shape=jax.ShapeDtypeStruct((nt, SUBLANES, LANES), F32),
        scratch_shapes=[pltpu.VMEM((nk, LANES), F32), pltpu.VMEM((nk, LANES), F32),
                        pltpu.VMEM((nk // SUBLANES, SUBLANES, LANES), F32),
                        pltpu.SemaphoreType.DMA((2,))] + [pltpu.SMEM((2 * tb,), I32)] * nk,
        compiler_params=_cparams("arbitrary", vmem=VMEM_LIMIT),
        name="peer_v",
    )(eidx, coef, tab)


def _final_kernel(last, x_ref, p_ref, mod_ref, nw_ref, o_ref):
    x2 = x_ref[0] + mod_ref[0, 5:6, :] * p_ref[0]
    if last:
        ms = jnp.mean(x2 * x2, axis=-1, keepdims=True)
        x2 = x2 * lax.rsqrt(ms + EPS) * nw_ref[...]
    o_ref[0] = x2


def _final(x1, peer, mod6, nw, last, tm):
    bsz, seq, d = x1.shape
    blk = pl.BlockSpec((1, tm, d), lambda b, i: (b, i, 0))
    return pl.pallas_call(
        functools.partial(_final_kernel, last),
        grid=(bsz, seq // tm),
        in_specs=[blk, blk, pl.BlockSpec((1, 6, d), lambda b, i: (b, 0, 0)),
                  pl.BlockSpec((1, d), lambda b, i: (0, 0))],
        out_specs=blk,
        out_shape=jax.ShapeDtypeStruct((bsz, seq, d), F32),
        compiler_params=_cparams("arbitrary", "arbitrary", vmem=VMEM_LIMIT),
        name="final",
    )(x1, peer, mod6, nw.reshape(1, d))


def _tile(n, pref):
    t = min(n, pref)
    assert n % t == 0, (n, t)
    return t


def kernel(x, c, ada_w, ada_b, norm_mix_w, norm_ffn_w, w_in, w_out, hg_lower_bounds, hg_gnorm_w,
           s5_a_re, s5_a_im, s5_log_dt, s5_b_re, s5_b_im, s5_c_re, s5_c_im, s5_d, s5_glu_w, s5_glu_b,
           peer_wq, peer_keys1, peer_keys2, peer_u, peer_v, final_norm_w):
    bsz, seq, d = x.shape
    depth = ada_w.shape[0]
    nt = bsz * seq
    hg_w = HG_HEADS * HG_DK
    assert d == SUBLANES * LANES and seq % HG_CHUNK == 0
    assert w_in.shape[2] == 4 * hg_w + s5_a_re.shape[1] * S5_GROUP and 4 * hg_w % (s5_a_re.shape[1] * S5_GROUP) == 0
    s5_col = 4 * hg_w // (s5_a_re.shape[1] * S5_GROUP)

    for l in range(depth):
        mod6 = _mod(c, ada_w[l], ada_b[l]).reshape(bsz, 6, d)
        proj = _inproj(x, mod6, norm_mix_w[l], w_in[l].astype(BF16), _tile(seq, 512))
        o_hg = _hgrn2(proj, hg_lower_bounds, hg_gnorm_w[l], l, _tile(seq, 512))
        o_s5 = _s5(proj, s5_col, s5_a_re[l], s5_a_im[l], s5_log_dt[l], s5_b_re[l], s5_b_im[l],
                   s5_c_re[l], s5_c_im[l], s5_d[l], s5_glu_w[l], s5_glu_b[l], _tile(seq, 256))
        x1, h2, scores_t = _mixout(o_hg, o_s5, x, mod6, w_out[l], norm_ffn_w[l], peer_wq[l],
                                   peer_keys1[l], peer_keys2[l], _tile(seq, 256))
        eidx, gates = _topk(scores_t, _tile(seq, 256))
        rows = eidx.shape[1]
        eidx = jnp.swapaxes(eidx, 0, 1).reshape(rows, nt)
        gates = jnp.swapaxes(gates, 0, 1).reshape(rows, nt)
        tb = _tile(nt, 128)
        coef = _peer_u(eidx, h2.reshape(nt, SUBLANES, LANES), gates, _table_tiles(peer_u[l]), tb)
        peer = _peer_v(eidx, coef, _table_tiles(peer_v[l]), tb).reshape(bsz, seq, d)
        x = _final(x1, peer, mod6, final_norm_w, l == depth - 1, _tile(seq, 512))
    return x
```

```python
import functools
import math

import jax
import jax.numpy as jnp
from jax import lax
from jax.experimental import pallas as pl
from jax.experimental.pallas import tpu as pltpu

F32 = jnp.float32
BF16 = jnp.bfloat16
I32 = jnp.int32

EPS = 1e-6
HG_HEADS = 4
HG_DK = 128
HG_CHUNK = 64
HG_CHUNKS_PER_ITER = 2
HG_MAX_FACTORED_DECAY = 60.0
S5_GROUP = 16
S5_STATE = 64
PEER_HEADS = 8
PEER_NKEYS = 128
PEER_TOPK = 16
PEER_HALF = 128

LANES = 128
SUBLANES = 8
VMEM_LIMIT = 56 * 1024 * 1024

_HIGHEST = lax.Precision.HIGHEST
_SQRT_HALF = 0.7071067811865476


def _cparams(*sem, vmem=None):
    return pltpu.CompilerParams(dimension_semantics=sem, vmem_limit_bytes=vmem)


def _silu(x):
    return x * jax.nn.sigmoid(x)


def _gelu(x):
    return 0.5 * x * (1.0 + lax.erf(x * _SQRT_HALF))


def _bdot(a, b):
    return jnp.dot(a.astype(BF16), b.astype(BF16), preferred_element_type=F32)


def _mod_kernel(c_ref, w_ref, b_ref, o_ref):
    cond = _silu(c_ref[...])
    o_ref[...] = jnp.dot(cond, w_ref[...], preferred_element_type=F32, precision=_HIGHEST) + b_ref[...]


def _mod(c, w, b):
    bsz, d = c.shape
    n = w.shape[1]
    return pl.pallas_call(
        _mod_kernel,
        grid=(n // d,),
        in_specs=[pl.BlockSpec((bsz, d), lambda j: (0, 0)),
                  pl.BlockSpec((d, d), lambda j: (0, j)),
                  pl.BlockSpec((1, d), lambda j: (0, j))],
        out_specs=pl.BlockSpec((bsz, d), lambda j: (0, j)),
        out_shape=jax.ShapeDtypeStruct((bsz, n), F32),
        compiler_params=_cparams("arbitrary"),
        name="mod",
    )(c, w, b.reshape(1, n))


def _norm_mod(x, nw, scale, shift):
    ms = jnp.mean(x * x, axis=-1, keepdims=True)
    return (x * lax.rsqrt(ms + EPS) * nw) * (1.0 + scale) + shift


def _inproj_kernel(x_ref, mod_ref, nw_ref, w_ref, o_ref):
    h = _norm_mod(x_ref[0], nw_ref[...], mod_ref[0, 1:2, :], mod_ref[0, 0:1, :])
    o_ref[0] = jnp.dot(h.astype(BF16), w_ref[...], preferred_element_type=F32)


def _inproj(x, mod6, nw, w_bf16, tm):
    bsz, seq, d = x.shape
    n = w_bf16.shape[1]
    return pl.pallas_call(
        _inproj_kernel,
        grid=(bsz, seq // tm),
        in_specs=[pl.BlockSpec((1, tm, d), lambda b, i: (b, i, 0)),
                  pl.BlockSpec((1, 6, d), lambda b, i: (b, 0, 0)),
                  pl.BlockSpec((1, d), lambda b, i: (0, 0)),
                  pl.BlockSpec((d, n), lambda b, i: (0, 0))],
        out_specs=pl.BlockSpec((1, tm, n), lambda b, i: (b, i, 0)),
        out_shape=jax.ShapeDtypeStruct((bsz, seq, n), F32),
        compiler_params=_cparams("arbitrary", "arbitrary", vmem=VMEM_LIMIT),
        name="inproj",
    )(x, mod6, nw.reshape(1, d), w_bf16)


def _hgrn2_kernel(layer, q_ref, f_ref, i_ref, g_ref, lbp_ref, gnw_ref, o_ref,
                  st_ref, b_s, qs_s, oi_s):
    c_len = HG_CHUNK
    n_chunks = q_ref.shape[1] // c_len

    @pl.when(pl.program_id(1) == 0)
    def _():
        st_ref[...] = jnp.zeros_like(st_ref)

    lbp = lbp_ref[...]
    e = jnp.exp(lbp - jnp.max(lbp, axis=0, keepdims=True))
    lb_all = jnp.sum(e[: layer + 1], axis=0, keepdims=True) / jnp.sum(e, axis=0, keepdims=True)
    gnw = gnw_ref[...]
    row = lax.broadcasted_iota(I32, (c_len, c_len), 0)
    col = lax.broadcasted_iota(I32, (c_len, c_len), 1)
    tril = (col <= row).astype(F32)
    s_iota = lax.broadcasted_iota(I32, (c_len, 1), 0)

    def chunk_group(gi, carry):
        units = []
        worst = None
        for c in range(HG_CHUNKS_PER_ITER):
            r0 = pl.multiple_of((gi * HG_CHUNKS_PER_ITER + c) * c_len, c_len)
            for h in range(HG_HEADS):
                cs = slice(h * HG_DK, (h + 1) * HG_DK)
                lb = lb_all[:, cs]
                f = lb + (1.0 - lb) * jax.nn.sigmoid(f_ref[0, pl.ds(r0, c_len), cs])
                kk = 1.0 - f
                qs = _silu(q_ref[0, pl.ds(r0, c_len), cs]) * (HG_DK ** -0.5)
                v = i_ref[0, pl.ds(r0, c_len), cs]
                b = jnp.dot(tril, jnp.log(f), preferred_element_type=F32, precision=_HIGHEST)
                units.append((r0, h, cs, kk, qs, v, b))
                decay = jnp.max(-b[c_len - 1:c_len, :])
                worst = decay if worst is None else jnp.maximum(worst, decay)

        def finish(unit, o_intra):
            r0, h, cs, kk, qs, v, b = unit
            st = st_ref[h]
            o_inter = lax.dot_general((qs * jnp.exp(b)).astype(BF16), st.astype(BF16),
                                      (((1,), (1,)), ((), ())), preferred_element_type=F32)
            o = o_inter + o_intra
            b_last = b[c_len - 1:c_len, :]
            kdec = kk * jnp.exp(b_last - b)
            upd = lax.dot_general(v.astype(BF16), kdec.astype(BF16),
                                  (((0,), (0,)), ((), ())), preferred_element_type=F32)
            st_ref[h] = st * jnp.exp(b_last) + upd
            y = o * lax.rsqrt(jnp.mean(o * o, axis=-1, keepdims=True) + EPS) * gnw
            o_ref[0, pl.ds(r0, c_len), cs] = y * _silu(g_ref[0, pl.ds(r0, c_len), cs])

        factorable = worst < HG_MAX_FACTORED_DECAY

        @pl.when(factorable)
        def _():
            for unit in units:
                _, _, _, kk, qs, v, b = unit
                qt = (qs * jnp.exp(b)).astype(BF16)
                kt = (kk * jnp.exp(-b)).astype(BF16)
                sc = lax.dot_general(qt, kt, (((1,), (1,)), ((), ())), preferred_element_type=F32)
                finish(unit, jnp.dot((sc * tril).astype(BF16), v.astype(BF16), preferred_element_type=F32))

        @pl.when(jnp.logical_not(factorable))
        def _():
            for unit in units:
                _, _, _, kk, qs, v, b = unit
                b_s[...] = b
                qs_s[...] = qs

                def row_step(t, c2, kk=kk, v=v, b=b):
                    bt = b_s[pl.ds(t, 1), :]
                    qt = qs_s[pl.ds(t, 1), :]
                    w = (qt * kk) * jnp.exp(jnp.minimum(bt - b, 0.0))
                    sc = jnp.sum(w, axis=1, keepdims=True)
                    sc = jnp.where(s_iota <= t, sc, 0.0)
                    oi_s[pl.ds(t, 1), :] = jnp.sum(sc * v, axis=0, keepdims=True)
                    return c2

                lax.fori_loop(0, c_len, row_step, 0)
                finish(unit, oi_s[...])

        return carry

    assert n_chunks % HG_CHUNKS_PER_ITER == 0
    lax.fori_loop(0, n_chunks // HG_CHUNKS_PER_ITER, chunk_group, 0)


def _hgrn2(proj, lbp, gnw, layer, tb):
    bsz, seq, _ = proj.shape
    w = HG_HEADS * HG_DK
    nl = lbp.shape[0]

    def col_spec(j):
        return pl.BlockSpec((1, tb, w), lambda b, i, j=j: (b, i, j))

    return pl.pallas_call(
        functools.partial(_hgrn2_kernel, layer),
        grid=(bsz, seq // tb),
        in_specs=[col_spec(0), col_spec(1), col_spec(2), col_spec(3),
                  pl.BlockSpec((nl, w), lambda b, i: (0, 0)),
                  pl.BlockSpec((1, HG_DK), lambda b, i: (0, 0))],
        out_specs=pl.BlockSpec((1, tb, w), lambda b, i: (b, i, 0)),
        out_shape=jax.ShapeDtypeStruct((bsz, seq, w), F32),
        scratch_shapes=[pltpu.VMEM((HG_HEADS, HG_DK, HG_DK), F32),
                        pltpu.VMEM((HG_CHUNK, HG_DK), F32),
                        pltpu.VMEM((HG_CHUNK, HG_DK), F32),
                        pltpu.VMEM((HG_CHUNK, HG_DK), F32)],
        compiler_params=_cparams("arbitrary", "arbitrary", vmem=VMEM_LIMIT),
        name="hgrn2",
    )(proj, proj, proj, proj, lbp, gnw.reshape(1, HG_DK))


def _s5prep_kernel(are_ref, aim_ref, ldt_ref, bre_ref, bim_ref, pre_ref, pim_ref, bbre_ref, bbim_ref):
    ar = are_ref[...]
    ai = aim_ref[...]
    dt = jnp.exp(ldt_ref[...])
    mag = jnp.exp(ar * dt)
    abr = mag * jnp.cos(ai * dt)
    abi = mag * jnp.sin(ai * dt)
    den = ar * ar + ai * ai
    nr = abr - 1.0
    ni = abi
    cre = (nr * ar + ni * ai) / den
    cim = (ni * ar - nr * ai) / den
    bre = bre_ref[...]
    bim = bim_ref[...]
    bbre_ref[...] = cre[:, None, :] * bre - cim[:, None, :] * bim
    bbim_ref[...] = cre[:, None, :] * bim + cim[:, None, :] * bre
    pr, pi = abr, abi
    pre_ref[0] = pr
    pim_ref[0] = pi
    for k in range(1, SUBLANES):
        pr, pi = pr * abr - pi * abi, pr * abi + pi * abr
        pre_ref[k] = pr
        pim_ref[k] = pi


def _s5prep(a_re, a_im, log_dt, b_re, b_im):
    g, p = a_re.shape
    hh = b_re.shape[2]
    out = pl.pallas_call(
        _s5prep_kernel,
        out_shape=(jax.ShapeDtypeStruct((SUBLANES, g, p), F32), jax.ShapeDtypeStruct((SUBLANES, g, p), F32),
                   jax.ShapeDtypeStruct((g, hh, p), F32), jax.ShapeDtypeStruct((g, hh, p), F32)),
        name="s5prep",
    )(a_re, a_im, log_dt.reshape(g, 1), jnp.swapaxes(b_re, 1, 2), jnp.swapaxes(b_im, 1, 2))
    return out


S5_COLS = 256
S5_UNROLL = 4
S5_DIAG = 2


def _s5_kernel(u_ref, w1_ref, t1r, t1i, t2r, t2i, t4r, t4i, pr_ref, pi_ref, w2r_ref, w2i_ref,
               d_ref, wg_ref, bg_ref, o_ref, xr_s, xi_s, cr_s, ci_s):
    n = xr_s.shape[1]
    rows = xr_s.shape[0]

    @pl.when(pl.program_id(1) == 0)
    def _():
        cr_s[...] = jnp.zeros_like(cr_s)
        ci_s[...] = jnp.zeros_like(ci_s)

    u = u_ref[0]
    ub = u.astype(BF16)
    w = u.shape[1]
    wq, nq = w // S5_DIAG, n // S5_DIAG
    for q in range(S5_DIAG):
        bu = jnp.dot(ub[:, q * wq:(q + 1) * wq], w1_ref[q], preferred_element_type=F32)
        xr_s[:, q * nq:(q + 1) * nq] = bu[:, :nq]
        xi_s[:, q * nq:(q + 1) * nq] = bu[:, nq:]

    for c0 in range(0, n, S5_COLS):
        cs = slice(c0, c0 + S5_COLS)
        steps = ((1, t1r[:, cs], t1i[:, cs]), (2, t2r[:, cs], t2i[:, cs]), (4, t4r[:, cs], t4i[:, cs]))
        pr = pr_ref[:, cs]
        pi = pi_ref[:, cs]

        def block(j, carry, cs=cs, steps=steps, pr=pr, pi=pi):
            cr, ci = carry
            local = []
            for q in range(S5_UNROLL):
                r0 = pl.multiple_of((j * S5_UNROLL + q) * SUBLANES, SUBLANES)
                xr = xr_s[pl.ds(r0, SUBLANES), cs]
                xi = xi_s[pl.ds(r0, SUBLANES), cs]
                for d, ar, ai in steps:
                    sr = pltpu.roll(xr, d, 0)
                    si = pltpu.roll(xi, d, 0)
                    xr, xi = xr + ar * sr - ai * si, xi + ar * si + ai * sr
                local.append((r0, xr, xi))
            last = SUBLANES - 1
            for r0, xr, xi in local:
                xr, xi = xr + pr * cr - pi * ci, xi + pr * ci + pi * cr
                xr_s[pl.ds(r0, SUBLANES), cs] = xr
                xi_s[pl.ds(r0, SUBLANES), cs] = xi
                cr = jnp.broadcast_to(xr[last:, :], xr.shape)
                ci = jnp.broadcast_to(xi[last:, :], xi.shape)
            return cr, ci

        cr, ci = lax.fori_loop(0, rows // (SUBLANES * S5_UNROLL), block, (cr_s[:, cs], ci_s[:, cs]))
        cr_s[:, cs] = cr
        ci_s[:, cs] = ci

    ys = []
    for q in range(S5_DIAG):
        ns = slice(q * nq, (q + 1) * nq)
        ys.append(jnp.dot(xr_s[:, ns].astype(BF16), w2r_ref[q], preferred_element_type=F32)
                  - jnp.dot(xi_s[:, ns].astype(BF16), w2i_ref[q], preferred_element_type=F32))
    y = jnp.concatenate(ys, axis=1)
    y = _gelu(y + d_ref[...] * u)
    z = jnp.dot(y.astype(BF16), wg_ref[...], preferred_element_type=F32) + bg_ref[...]
    o_ref[0] = y * jax.nn.sigmoid(z)


def _s5(proj, col_block, a_re, a_im, log_dt, b_re, b_im, c_re, c_im, d_skip, w_glu, b_glu, tb):
    bsz, seq, _ = proj.shape
    g, p = a_re.shape
    hh = b_re.shape[2]
    w = g * hh
    n = g * p
    pre, pim, bbre, bbim = _s5prep(a_re, a_im, log_dt, b_re, b_im)
    gq = g // S5_DIAG
    eye = jnp.eye(gq, dtype=F32)

    def blockdiag(m):
        a, b = m.shape[1], m.shape[2]
        m = m.reshape(S5_DIAG, gq, a, b)
        return (eye[None, :, None, :, None] * m[:, :, :, None, :]).reshape(S5_DIAG, gq * a, gq * b)

    w1 = jnp.concatenate([blockdiag(bbre), blockdiag(bbim)], axis=2).astype(BF16)
    w2r = blockdiag(jnp.swapaxes(c_re, 1, 2)).astype(BF16)
    w2i = blockdiag(jnp.swapaxes(c_im, 1, 2)).astype(BF16)
    pre = pre.reshape(SUBLANES, n)
    pim = pim.reshape(SUBLANES, n)
    rid = jnp.arange(SUBLANES)[:, None]

    def step_table(d):
        return (jnp.where(rid >= d, pre[d - 1][None, :], 0.0), jnp.where(rid >= d, pim[d - 1][None, :], 0.0))

    t1r, t1i = step_table(1)
    t2r, t2i = step_table(2)
    t4r, t4i = step_table(4)

    def full(a):
        return pl.BlockSpec(a.shape, lambda b, i: (0,) * a.ndim)

    consts = [w1, t1r, t1i, t2r, t2i, t4r, t4i, pre, pim, w2r, w2i,
              d_skip.reshape(1, w), w_glu.astype(BF16), b_glu.reshape(1, w)]
    return pl.pallas_call(
        _s5_kernel,
        grid=(bsz, seq // tb),
        in_specs=[pl.BlockSpec((1, tb, w), lambda b, i: (b, i, col_block))] + [full(a) for a in consts],
        out_specs=pl.BlockSpec((1, tb, w), lambda b, i: (b, i, 0)),
        out_shape=jax.ShapeDtypeStruct((bsz, seq, w), F32),
        scratch_shapes=[pltpu.VMEM((tb, n), F32), pltpu.VMEM((tb, n), F32),
                        pltpu.VMEM((SUBLANES, n), F32), pltpu.VMEM((SUBLANES, n), F32)],
        compiler_params=_cparams("arbitrary", "arbitrary", vmem=VMEM_LIMIT),
        name="s5",
    )(proj, *consts)


def _mixout_kernel(ohg_ref, os5_ref, x_ref, mod_ref, wo1_ref, wo2_ref, nw_ref, wq_ref, k1_ref, k2_ref,
                   x1_ref, h2_ref, sc_ref):
    mixed = (jnp.dot(ohg_ref[0].astype(BF16), wo1_ref[...], preferred_element_type=F32)
             + jnp.dot(os5_ref[0].astype(BF16), wo2_ref[...], preferred_element_type=F32))
    x1 = x_ref[0] + mod_ref[0, 2:3, :] * mixed
    x1_ref[0] = x1
    h2 = _norm_mod(x1, nw_ref[...], mod_ref[0, 4:5, :], mod_ref[0, 3:4, :])
    for j in range(SUBLANES):
        h2_ref[0, :, j, :] = h2[:, j * LANES:(j + 1) * LANES]
    q = jnp.dot(h2.astype(BF16), wq_ref[...], preferred_element_type=F32)
    for m in range(2 * PEER_HEADS):
        keys = k1_ref[...] if m % 2 == 0 else k2_ref[...]
        qh = q[:, m * PEER_HALF:(m + 1) * PEER_HALF].astype(BF16)
        sc_ref[0, m] = lax.dot_general(keys, qh, (((1,), (1,)), ((), ())), preferred_element_type=F32)


def _mixout(o_hg, o_s5, x, mod6, w_out, nw, w_q, keys1, keys2, tm):
    bsz, seq, d = x.shape
    w = o_hg.shape[2]
    nq = w_q.shape[1]
    wo = w_out.astype(BF16)

    def full(a):
        return pl.BlockSpec(a.shape, lambda b, i: (0,) * a.ndim)

    consts_a = [wo[:w], wo[w:]]
    consts_b = [nw.reshape(1, d), w_q.astype(BF16), keys1.astype(BF16), keys2.astype(BF16)]
    return pl.pallas_call(
        _mixout_kernel,
        grid=(bsz, seq // tm),
        in_specs=[pl.BlockSpec((1, tm, w), lambda b, i: (b, i, 0)),
                  pl.BlockSpec((1, tm, w), lambda b, i: (b, i, 0)),
                  pl.BlockSpec((1, tm, d), lambda b, i: (b, i, 0)),
                  pl.BlockSpec((1, 6, d), lambda b, i: (b, 0, 0))]
                 + [full(a) for a in consts_a] + [full(a) for a in consts_b],
        out_specs=[pl.BlockSpec((1, tm, d), lambda b, i: (b, i, 0)),
                   pl.BlockSpec((1, tm, SUBLANES, LANES), lambda b, i: (b, i, 0, 0)),
                   pl.BlockSpec((1, 2 * PEER_HEADS, PEER_NKEYS, tm), lambda b, i: (b, 0, 0, i))],
        out_shape=(jax.ShapeDtypeStruct((bsz, seq, d), F32),
                   jax.ShapeDtypeStruct((bsz, seq, SUBLANES, LANES), F32),
                   jax.ShapeDtypeStruct((bsz, 2 * PEER_HEADS, PEER_NKEYS, seq), F32)),
        compiler_params=_cparams("arbitrary", "arbitrary", vmem=VMEM_LIMIT),
        name="mixout",
    )(o_hg, o_s5, x, mod6, *consts_a, *consts_b)


def _topk_rows(s, k):
    rows, tt = s.shape
    row = lax.broadcasted_iota(I32, (SUBLANES, tt), 0)
    tiles = [s[j:j + SUBLANES] for j in range(0, rows, SUBLANES)]
    ids = [row + j for j in range(0, rows, SUBLANES)]
    vals, idxs = [], []
    for it in range(k):
        lvl = list(zip(tiles, ids))
        while len(lvl) > 1:
            nxt = []
            for (a, ia), (b, ib) in zip(lvl[0::2], lvl[1::2]):
                take = a >= b
                nxt.append((jnp.where(take, a, b), jnp.where(take, ia, ib)))
            lvl = nxt
        best, best_id = lvl[0]
        m = jnp.max(best, axis=0, keepdims=True)
        am = jnp.min(jnp.where(best == m, best_id, rows), axis=0, keepdims=True)
        vals.append(m)
        idxs.append(am)
        if it + 1 < k:
            tiles = [jnp.where(i == am, -jnp.inf, t) for t, i in zip(tiles, ids)]
    return vals, idxs


def _topk_kernel(sc_ref, e_ref, g_ref):
    k = PEER_TOPK
    tt = sc_ref.shape[3]
    assert k == 2 * SUBLANES

    def head(h, carry):
        v1, i1 = _topk_rows(sc_ref[0, 2 * h], k)
        v2, i2 = _topk_rows(sc_ref[0, 2 * h + 1], k)
        v1a = jnp.concatenate(v1, axis=0)
        i1a = jnp.concatenate(i1, axis=0)
        v2a = jnp.concatenate(v2, axis=0)
        i2a = jnp.concatenate(i2, axis=0)
        row = lax.broadcasted_iota(I32, (SUBLANES, tt), 0)
        cands, cidxs, cposs = [], [], []
        tail = k // 2
        for a in range(tail):
            nb = k // (a + 1)
            for b0 in range(0, nb, SUBLANES):
                c = v1[a] + v2a[b0:b0 + SUBLANES]
                if nb - b0 < SUBLANES:
                    c = jnp.where(row < nb - b0, c, -jnp.inf)
                cands.append(c)
                cidxs.append(i1[a] * PEER_NKEYS + i2a[b0:b0 + SUBLANES])
                cposs.append(a * k + b0 + row)
        cands.append(v1a[tail:] + v2[0])
        cidxs.append(i1a[tail:] * PEER_NKEYS + i2[0])
        cposs.append((tail + row) * k)

        def over_tiles(op, tiles):
            return functools.reduce(op, tiles)

        best, eidx = [], []
        for _ in range(k):
            m = jnp.max(over_tiles(jnp.maximum, cands), axis=0, keepdims=True)
            pos = jnp.min(over_tiles(jnp.minimum, [jnp.where(c == m, cp, k * k) for c, cp in zip(cands, cposs)]),
                          axis=0, keepdims=True)
            hits = [cp == pos for cp in cposs]
            eidx.append(jnp.max(over_tiles(jnp.maximum, [jnp.where(hh, ci, -1) for hh, ci in zip(hits, cidxs)]),
                                axis=0, keepdims=True))
            best.append(m)
            cands = [jnp.where(hh, -jnp.inf, c) for hh, c in zip(hits, cands)]
        best = jnp.concatenate(best, axis=0)
        ex = jnp.exp(best - best[0:1, :])
        r0 = pl.multiple_of(h * k, k)
        g_ref[pl.ds(r0, k), :] = ex / jnp.sum(ex, axis=0, keepdims=True)
        e_ref[pl.ds(r0, k), :] = jnp.concatenate(eidx, axis=0)
        return carry

    lax.fori_loop(0, PEER_HEADS, head, 0)


def _topk(scores_t, tt):
    bsz, m, nk, seq = scores_t.shape
    rows = PEER_HEADS * PEER_TOPK
    nblk = seq // tt
    out_blk = pl.BlockSpec((rows, tt), lambda b, i: (0, b * nblk + i))
    return pl.pallas_call(
        _topk_kernel,
        grid=(bsz, nblk),
        in_specs=[pl.BlockSpec((1, m, nk, tt), lambda b, i: (b, 0, 0, i))],
        out_specs=[out_blk, out_blk],
        out_shape=(jax.ShapeDtypeStruct((rows, bsz * seq), I32),
                   jax.ShapeDtypeStruct((rows, bsz * seq), F32)),
        compiler_params=_cparams("arbitrary", "arbitrary", vmem=VMEM_LIMIT),
        name="topk",
    )(scores_t)


def _table_tiles(tab):
    e, d = tab.shape
    return tab.astype(BF16).reshape(e, d // LANES, LANES)


def _sublane_sums(p):
    row = lax.broadcasted_iota(I32, (SUBLANES, LANES), 0)
    m4 = row < 4
    m2 = (row & 3) < 2
    m1 = (row & 1) < 1

    def fold4(a, b):
        return jnp.where(m4, a, b) + pltpu.roll(jnp.where(m4, b, a), 4, 0)

    def fold(c1, c2, m, d):
        return (jnp.where(m, c1, pltpu.roll(c2, d, 0))
                + jnp.where(m, pltpu.roll(c1, SUBLANES - d, 0), c2))

    d1 = fold(fold4(p[0], p[4]), fold4(p[2], p[6]), m2, 2)
    d2 = fold(fold4(p[1], p[5]), fold4(p[3], p[7]), m2, 2)
    return fold(d1, d2, m1, 1)


def _stage_ids(idx_hbm, id_refs, sems, tb):
    i = pl.program_id(0)

    def copies(blk, buf):
        return [pltpu.make_async_copy(idx_hbm.at[k, pl.ds(blk * tb, tb)], ref.at[pl.ds(buf * tb, tb)], sems.at[buf])
                for k, ref in enumerate(id_refs)]

    @pl.when(i == 0)
    def _():
        for cp in copies(0, 0):
            cp.start()

    @pl.when(i + 1 < pl.num_programs(0))
    def _():
        for cp in copies(i + 1, (i + 1) % 2):
            cp.start()

    for cp in copies(i, i % 2):
        cp.wait()
    return (i % 2) * tb


def _peer_u_kernel(idx_hbm, h_ref, g_ref, tab_ref, o_ref, acts_s, red_s, sems, *id_refs):
    nk, tb = g_ref.shape
    groups = nk // SUBLANES
    lane = lax.broadcasted_iota(I32, (SUBLANES, tb), 1)
    id0 = _stage_ids(idx_hbm, id_refs, sems, tb)
    acts_s[...] = jnp.zeros_like(acts_s)
    red_s[...] = jnp.zeros_like(red_s)

    def finish(t):
        for g in range(groups):
            rows = slice(g * SUBLANES, (g + 1) * SUBLANES)
            col = jnp.sum(red_s[g], axis=1, keepdims=True)
            acts_s[rows, :] = jnp.where(lane == t, col, acts_s[rows, :])

    def token(t, carry):
        finish(t - 1)
        h = h_ref[t]
        ti = id0 + t
        for g in range(groups):
            p = [tab_ref[id_refs[g * SUBLANES + i][ti]].astype(F32) * h for i in range(SUBLANES)]
            red_s[g] = _sublane_sums(p)
        return carry

    lax.fori_loop(0, tb, token, 0)
    finish(tb - 1)
    o_ref[...] = g_ref[...] * _gelu(acts_s[...])


def _peer_u(eidx, h2r, gates, tab, tb):
    nk, nt = gates.shape
    return pl.pallas_call(
        _peer_u_kernel,
        grid=(nt // tb,),
        in_specs=[pl.BlockSpec(memory_space=pl.ANY),
                  pl.BlockSpec((tb, SUBLANES, LANES), lambda i: (i, 0, 0)),
                  pl.BlockSpec((nk, tb), lambda i: (0, i)),
                  pl.BlockSpec(tab.shape, lambda i: (0, 0, 0), pipeline_mode=pl.Buffered(1))],
        out_specs=pl.BlockSpec((nk, tb), lambda i: (0, i)),
        out_shape=jax.ShapeDtypeStruct((nk, nt), F32),
        scratch_shapes=[pltpu.VMEM((nk, tb), F32), pltpu.VMEM((nk // SUBLANES, SUBLANES, LANES), F32),
                        pltpu.SemaphoreType.DMA((2,))] + [pltpu.SMEM((2 * tb,), I32)] * nk,
        compiler_params=_cparams("arbitrary", vmem=VMEM_LIMIT),
        name="peer_u",
    )(eidx, h2r, gates, tab)


def _peer_v_kernel(idx_hbm, c_ref, tab_ref, o_ref, cb0_s, cb1_s, part_s, sems, *id_refs):
    nk, tb = c_ref.shape
    groups = nk // SUBLANES
    lane = lax.broadcasted_iota(I32, (SUBLANES, tb), 1)
    id0 = _stage_ids(idx_hbm, id_refs, sems, tb)

    def prep(t, cb):
        for g in range(groups):
            rows = slice(g * SUBLANES, (g + 1) * SUBLANES)
            col = jnp.sum(jnp.where(lane == t, c_ref[rows, :], 0.0), axis=1, keepdims=True)
            cb[rows, :] = jnp.broadcast_to(col, (SUBLANES, LANES))

    def gather(t, cb):
        ti = id0 + t
        for g in range(groups):
            p = [cb[k:k + 1, :] * tab_ref[id_refs[k][ti]].astype(F32)
                 for k in range(g * SUBLANES, (g + 1) * SUBLANES)]
            part_s[g] = ((p[0] + p[1]) + (p[2] + p[3])) + ((p[4] + p[5]) + (p[6] + p[7]))

    def finish(t):
        s = [part_s[g] for g in range(groups)]
        while len(s) > 1:
            s = [s[i] + s[i + 1] for i in range(0, len(s), 2)]
        o_ref[t] = s[0]

    prep(0, cb0_s)
    part_s[...] = jnp.zeros_like(part_s)

    def token(t, carry):
        prep(t + 1, cb1_s)
        finish(jnp.maximum(t - 1, 0))
        gather(t, cb0_s)
        cb0_s[...] = cb1_s[...]
        return carry

    lax.fori_loop(0, tb, token, 0)
    finish(tb - 1)


def _peer_v(eidx, coef, tab, tb):
    nk, nt = coef.shape
    return pl.pallas_call(
        _peer_v_kernel,
        grid=(nt // tb,),
        in_specs=[pl.BlockSpec(memory_space=pl.ANY),
                  pl.BlockSpec((nk, tb), lambda i: (0, i)),
                  pl.BlockSpec(tab.shape, lambda i: (0, 0, 0), pipeline_mode=pl.Buffered(1))],
        out_specs=pl.BlockSpec((tb, SUBLANES, LANES), lambda i: (i, 0, 0)),
        out_shape=jax.ShapeDtypeStruct((nt, SUBLANES, LANES), F32),
        scratch_shapes=[pltpu.VMEM((nk, LANES), F32), pltpu.VMEM((nk, LANES), F32),
                        pltpu.VMEM((nk // SUBLANES, SUBLANES, LANES), F32),
                        pltpu.SemaphoreType.DMA((2,))] + [pltpu.SMEM((2 * tb,), I32)] * nk,
        compiler_params=_cparams("arbitrary", vmem=VMEM_LIMIT),
        name="peer_v",
    )(eidx, coef, tab)


def _final_kernel(last, x_ref, p_ref, mod_ref, nw_ref, o_ref):
    peer = jnp.concatenate([p_ref[0, :, j, :] for j in range(SUBLANES)], axis=1)
    x2 = x_ref[0] + mod_ref[0, 5:6, :] * peer
    if last:
        ms = jnp.mean(x2 * x2, axis=-1, keepdims=True)
        x2 = x2 * lax.rsqrt(ms + EPS) * nw_ref[...]
    o_ref[0] = x2


def _final(x1, peer_tiles, mod6, nw, last, tm):
    bsz, seq, d = x1.shape
    blk = pl.BlockSpec((1, tm, d), lambda b, i: (b, i, 0))
    return pl.pallas_call(
        functools.partial(_final_kernel, last),
        grid=(bsz, seq // tm),
        in_specs=[blk, pl.BlockSpec((1, tm, SUBLANES, LANES), lambda b, i: (b, i, 0, 0)),
                  pl.BlockSpec((1, 6, d), lambda b, i: (b, 0, 0)),
                  pl.BlockSpec((1, d), lambda b, i: (0, 0))],
        out_specs=blk,
        out_shape=jax.ShapeDtypeStruct((bsz, seq, d), F32),
        compiler_params=_cparams("arbitrary", "arbitrary", vmem=VMEM_LIMIT),
        name="final",
    )(x1, peer_tiles, mod6, nw.reshape(1, d))


def _tile(n, pref):
    t = min(n, pref)
    assert n % t == 0, (n, t)
    return t


def _token_tiles(seq, nt):
    return {
        "inproj": _tile(seq, 512),
        "hgrn2": _tile(seq, 512),
        "s5": _tile(seq, 256),
        "mixout": _tile(seq, 256),
        "topk": _tile(seq, 512),
        "peer": _tile(nt, 128),
        "final": _tile(seq, 512),
    }


def kernel(x, c, ada_w, ada_b, norm_mix_w, norm_ffn_w, w_in, w_out, hg_lower_bounds, hg_gnorm_w,
           s5_a_re, s5_a_im, s5_log_dt, s5_b_re, s5_b_im, s5_c_re, s5_c_im, s5_d, s5_glu_w, s5_glu_b,
           peer_wq, peer_keys1, peer_keys2, peer_u, peer_v, final_norm_w):
    bsz, seq, d = x.shape
    depth = ada_w.shape[0]
    nt = bsz * seq
    hg_w = HG_HEADS * HG_DK
    assert d == SUBLANES * LANES and seq % HG_CHUNK == 0
    assert w_in.shape[2] == 4 * hg_w + s5_a_re.shape[1] * S5_GROUP and 4 * hg_w % (s5_a_re.shape[1] * S5_GROUP) == 0
    s5_col = 4 * hg_w // (s5_a_re.shape[1] * S5_GROUP)

    tiles = _token_tiles(seq, nt)
    for l in range(depth):
        mod6 = _mod(c, ada_w[l], ada_b[l]).reshape(bsz, 6, d)
        proj = _inproj(x, mod6, norm_mix_w[l], w_in[l].astype(BF16), tiles["inproj"])
        o_hg = _hgrn2(proj, hg_lower_bounds, hg_gnorm_w[l], l, tiles["hgrn2"])
        o_s5 = _s5(proj, s5_col, s5_a_re[l], s5_a_im[l], s5_log_dt[l], s5_b_re[l], s5_b_im[l],
                   s5_c_re[l], s5_c_im[l], s5_d[l], s5_glu_w[l], s5_glu_b[l], tiles["s5"])
        x1, h2_tiles, scores_t = _mixout(o_hg, o_s5, x, mod6, w_out[l], norm_ffn_w[l], peer_wq[l],
                                         peer_keys1[l], peer_keys2[l], tiles["mixout"])
        eidx, gates = _topk(scores_t, tiles["topk"])
        coef = _peer_u(eidx, h2_tiles.reshape(nt, SUBLANES, LANES), gates, _table_tiles(peer_u[l]), tiles["peer"])
        peer_tiles = _peer_v(eidx, coef, _table_tiles(peer_v[l]), tiles["peer"])
        x = _final(x1, peer_tiles.reshape(bsz, seq, SUBLANES, LANES), mod6, final_norm_w, l == depth - 1,
                   tiles["final"])
    return x
```

```python
import functools
import math

import jax
import jax.numpy as jnp
from jax import lax
from jax.experimental import pallas as pl
from jax.experimental.pallas import tpu as pltpu

F32 = jnp.float32
BF16 = jnp.bfloat16
I32 = jnp.int32

EPS = 1e-6
HG_HEADS = 4
HG_DK = 128
HG_CHUNK = 64
HG_CHUNKS_PER_ITER = 4
HG_MAX_FACTORED_DECAY = 60.0
S5_GROUP = 16
S5_STATE = 64
PEER_HEADS = 8
PEER_NKEYS = 128
PEER_TOPK = 16
PEER_HALF = 128

LANES = 128
SUBLANES = 8
VMEM_LIMIT = 56 * 1024 * 1024

_HIGHEST = lax.Precision.HIGHEST
_SQRT_HALF = 0.7071067811865476


def _cparams(*sem, vmem=None):
    return pltpu.CompilerParams(dimension_semantics=sem, vmem_limit_bytes=vmem)


def _silu(x):
    return x * jax.nn.sigmoid(x)


def _gelu(x):
    return 0.5 * x * (1.0 + lax.erf(x * _SQRT_HALF))


def _bdot(a, b):
    return jnp.dot(a.astype(BF16), b.astype(BF16), preferred_element_type=F32)


def _mod_kernel(c_ref, w_ref, b_ref, o_ref):
    cond = _silu(c_ref[...])
    o_ref[...] = jnp.dot(cond, w_ref[...], preferred_element_type=F32, precision=_HIGHEST) + b_ref[...]


def _mod(c, w, b):
    bsz, d = c.shape
    n = w.shape[1]
    return pl.pallas_call(
        _mod_kernel,
        grid=(n // d,),
        in_specs=[pl.BlockSpec((bsz, d), lambda j: (0, 0)),
                  pl.BlockSpec((d, d), lambda j: (0, j)),
                  pl.BlockSpec((1, d), lambda j: (0, j))],
        out_specs=pl.BlockSpec((bsz, d), lambda j: (0, j)),
        out_shape=jax.ShapeDtypeStruct((bsz, n), F32),
        compiler_params=_cparams("arbitrary"),
        name="mod",
    )(c, w, b.reshape(1, n))


def _norm_mod(x, nw, scale, shift):
    ms = jnp.mean(x * x, axis=-1, keepdims=True)
    return (x * lax.rsqrt(ms + EPS) * nw) * (1.0 + scale) + shift


def _inproj_kernel(x_ref, mod_ref, nw_ref, w_ref, o_ref):
    h = _norm_mod(x_ref[0], nw_ref[...], mod_ref[0, 1:2, :], mod_ref[0, 0:1, :])
    o_ref[0] = jnp.dot(h.astype(BF16), w_ref[...], preferred_element_type=F32)


def _inproj(x, mod6, nw, w_bf16, tm):
    bsz, seq, d = x.shape
    n = w_bf16.shape[1]
    return pl.pallas_call(
        _inproj_kernel,
        grid=(bsz, seq // tm),
        in_specs=[pl.BlockSpec((1, tm, d), lambda b, i: (b, i, 0)),
                  pl.BlockSpec((1, 6, d), lambda b, i: (b, 0, 0)),
                  pl.BlockSpec((1, d), lambda b, i: (0, 0)),
                  pl.BlockSpec((d, n), lambda b, i: (0, 0))],
        out_specs=pl.BlockSpec((1, tm, n), lambda b, i: (b, i, 0)),
        out_shape=jax.ShapeDtypeStruct((bsz, seq, n), F32),
        compiler_params=_cparams("arbitrary", "arbitrary", vmem=VMEM_LIMIT),
        name="inproj",
    )(x, mod6, nw.reshape(1, d), w_bf16)


def _hgrn2_kernel(layer, q_ref, f_ref, i_ref, g_ref, lbp_ref, gnw_ref, o_ref,
                  st_ref, b_s, qs_s, oi_s):
    c_len = HG_CHUNK
    n_chunks = q_ref.shape[1] // c_len

    @pl.when(pl.program_id(1) == 0)
    def _():
        st_ref[...] = jnp.zeros_like(st_ref)

    lbp = lbp_ref[...]
    e = jnp.exp(lbp - jnp.max(lbp, axis=0, keepdims=True))
    lb_all = jnp.sum(e[: layer + 1], axis=0, keepdims=True) / jnp.sum(e, axis=0, keepdims=True)
    gnw = gnw_ref[...]
    row = lax.broadcasted_iota(I32, (c_len, c_len), 0)
    col = lax.broadcasted_iota(I32, (c_len, c_len), 1)
    tril = (col <= row).astype(F32)
    s_iota = lax.broadcasted_iota(I32, (c_len, 1), 0)

    def chunk_group(gi, carry):
        units = []
        worst = None
        for c in range(HG_CHUNKS_PER_ITER):
            r0 = pl.multiple_of((gi * HG_CHUNKS_PER_ITER + c) * c_len, c_len)
            for h in range(HG_HEADS):
                cs = slice(h * HG_DK, (h + 1) * HG_DK)
                lb = lb_all[:, cs]
                f = lb + (1.0 - lb) * jax.nn.sigmoid(f_ref[0, pl.ds(r0, c_len), cs])
                kk = 1.0 - f
                qs = _silu(q_ref[0, pl.ds(r0, c_len), cs]) * (HG_DK ** -0.5)
                v = i_ref[0, pl.ds(r0, c_len), cs]
                b = jnp.dot(tril, jnp.log(f), preferred_element_type=F32, precision=_HIGHEST)
                units.append((r0, h, cs, kk, qs, v, b))
                decay = jnp.max(-b[c_len - 1:c_len, :])
                worst = decay if worst is None else jnp.maximum(worst, decay)

        def finish(unit, o_intra):
            r0, h, cs, kk, qs, v, b = unit
            st = st_ref[h]
            o_inter = lax.dot_general((qs * jnp.exp(b)).astype(BF16), st.astype(BF16),
                                      (((1,), (1,)), ((), ())), preferred_element_type=F32)
            o = o_inter + o_intra
            b_last = b[c_len - 1:c_len, :]
            kdec = kk * jnp.exp(b_last - b)
            upd = lax.dot_general(v.astype(BF16), kdec.astype(BF16),
                                  (((0,), (0,)), ((), ())), preferred_element_type=F32)
            st_ref[h] = st * jnp.exp(b_last) + upd
            y = o * lax.rsqrt(jnp.mean(o * o, axis=-1, keepdims=True) + EPS) * gnw
            o_ref[0, pl.ds(r0, c_len), cs] = y * _silu(g_ref[0, pl.ds(r0, c_len), cs])

        factorable = worst < HG_MAX_FACTORED_DECAY

        @pl.when(factorable)
        def _():
            for unit in units:
                _, _, _, kk, qs, v, b = unit
                qt = (qs * jnp.exp(b)).astype(BF16)
                kt = (kk * jnp.exp(-b)).astype(BF16)
                sc = lax.dot_general(qt, kt, (((1,), (1,)), ((), ())), preferred_element_type=F32)
                finish(unit, jnp.dot((sc * tril).astype(BF16), v.astype(BF16), preferred_element_type=F32))

        @pl.when(jnp.logical_not(factorable))
        def _():
            for unit in units:
                _, _, _, kk, qs, v, b = unit
                b_s[...] = b
                qs_s[...] = qs

                def row_step(t, c2, kk=kk, v=v, b=b):
                    bt = b_s[pl.ds(t, 1), :]
                    qt = qs_s[pl.ds(t, 1), :]
                    w = (qt * kk) * jnp.exp(jnp.minimum(bt - b, 0.0))
                    sc = jnp.sum(w, axis=1, keepdims=True)
                    sc = jnp.where(s_iota <= t, sc, 0.0)
                    oi_s[pl.ds(t, 1), :] = jnp.sum(sc * v, axis=0, keepdims=True)
                    return c2

                lax.fori_loop(0, c_len, row_step, 0)
                finish(unit, oi_s[...])

        return carry

    assert n_chunks % HG_CHUNKS_PER_ITER == 0
    lax.fori_loop(0, n_chunks // HG_CHUNKS_PER_ITER, chunk_group, 0)


def _hgrn2(proj, lbp, gnw, layer, tb):
    bsz, seq, _ = proj.shape
    w = HG_HEADS * HG_DK
    nl = lbp.shape[0]

    def col_spec(j):
        return pl.BlockSpec((1, tb, w), lambda b, i, j=j: (b, i, j))

    return pl.pallas_call(
        functools.partial(_hgrn2_kernel, layer),
        grid=(bsz, seq // tb),
        in_specs=[col_spec(0), col_spec(1), col_spec(2), col_spec(3),
                  pl.BlockSpec((nl, w), lambda b, i: (0, 0)),
                  pl.BlockSpec((1, HG_DK), lambda b, i: (0, 0))],
        out_specs=pl.BlockSpec((1, tb, w), lambda b, i: (b, i, 0)),
        out_shape=jax.ShapeDtypeStruct((bsz, seq, w), F32),
        scratch_shapes=[pltpu.VMEM((HG_HEADS, HG_DK, HG_DK), F32),
                        pltpu.VMEM((HG_CHUNK, HG_DK), F32),
                        pltpu.VMEM((HG_CHUNK, HG_DK), F32),
                        pltpu.VMEM((HG_CHUNK, HG_DK), F32)],
        compiler_params=_cparams("arbitrary", "arbitrary", vmem=VMEM_LIMIT),
        name="hgrn2",
    )(proj, proj, proj, proj, lbp, gnw.reshape(1, HG_DK))


def _s5prep_kernel(are_ref, aim_ref, ldt_ref, bre_ref, bim_ref, pre_ref, pim_ref, bbre_ref, bbim_ref):
    ar = are_ref[...]
    ai = aim_ref[...]
    dt = jnp.exp(ldt_ref[...])
    mag = jnp.exp(ar * dt)
    abr = mag * jnp.cos(ai * dt)
    abi = mag * jnp.sin(ai * dt)
    den = ar * ar + ai * ai
    nr = abr - 1.0
    ni = abi
    cre = (nr * ar + ni * ai) / den
    cim = (ni * ar - nr * ai) / den
    bre = bre_ref[...]
    bim = bim_ref[...]
    bbre_ref[...] = cre[:, None, :] * bre - cim[:, None, :] * bim
    bbim_ref[...] = cre[:, None, :] * bim + cim[:, None, :] * bre
    pr, pi = abr, abi
    pre_ref[0] = pr
    pim_ref[0] = pi
    for k in range(1, SUBLANES):
        pr, pi = pr * abr - pi * abi, pr * abi + pi * abr
        pre_ref[k] = pr
        pim_ref[k] = pi


def _s5prep(a_re, a_im, log_dt, b_re, b_im):
    g, p = a_re.shape
    hh = b_re.shape[2]
    out = pl.pallas_call(
        _s5prep_kernel,
        out_shape=(jax.ShapeDtypeStruct((SUBLANES, g, p), F32), jax.ShapeDtypeStruct((SUBLANES, g, p), F32),
                   jax.ShapeDtypeStruct((g, hh, p), F32), jax.ShapeDtypeStruct((g, hh, p), F32)),
        name="s5prep",
    )(a_re, a_im, log_dt.reshape(g, 1), jnp.swapaxes(b_re, 1, 2), jnp.swapaxes(b_im, 1, 2))
    return out


S5_COLS = 256
S5_UNROLL = 4
S5_DIAG = 2


def _s5_kernel(u_ref, w1_ref, t1r, t1i, t2r, t2i, t4r, t4i, pr_ref, pi_ref, w2r_ref, w2i_ref,
               d_ref, wg_ref, bg_ref, o_ref, xr_s, xi_s, cr_s, ci_s):
    n = xr_s.shape[1]
    rows = xr_s.shape[0]

    @pl.when(pl.program_id(1) == 0)
    def _():
        cr_s[...] = jnp.zeros_like(cr_s)
        ci_s[...] = jnp.zeros_like(ci_s)

    u = u_ref[0]
    ub = u.astype(BF16)
    w = u.shape[1]
    wq, nq = w // S5_DIAG, n // S5_DIAG
    for q in range(S5_DIAG):
        bu = jnp.dot(ub[:, q * wq:(q + 1) * wq], w1_ref[q], preferred_element_type=F32)
        xr_s[:, q * nq:(q + 1) * nq] = bu[:, :nq]
        xi_s[:, q * nq:(q + 1) * nq] = bu[:, nq:]

    for c0 in range(0, n, S5_COLS):
        cs = slice(c0, c0 + S5_COLS)
        steps = ((1, t1r[:, cs], t1i[:, cs]), (2, t2r[:, cs], t2i[:, cs]), (4, t4r[:, cs], t4i[:, cs]))
        pr = pr_ref[:, cs]
        pi = pi_ref[:, cs]

        def block(j, carry, cs=cs, steps=steps, pr=pr, pi=pi):
            cr, ci = carry
            local = []
            for q in range(S5_UNROLL):
                r0 = pl.multiple_of((j * S5_UNROLL + q) * SUBLANES, SUBLANES)
                xr = xr_s[pl.ds(r0, SUBLANES), cs]
                xi = xi_s[pl.ds(r0, SUBLANES), cs]
                for d, ar, ai in steps:
                    sr = pltpu.roll(xr, d, 0)
                    si = pltpu.roll(xi, d, 0)
                    xr, xi = xr + ar * sr - ai * si, xi + ar * si + ai * sr
                local.append((r0, xr, xi))
            last = SUBLANES - 1
            for r0, xr, xi in local:
                xr, xi = xr + pr * cr - pi * ci, xi + pr * ci + pi * cr
                xr_s[pl.ds(r0, SUBLANES), cs] = xr
                xi_s[pl.ds(r0, SUBLANES), cs] = xi
                cr = jnp.broadcast_to(xr[last:, :], xr.shape)
                ci = jnp.broadcast_to(xi[last:, :], xi.shape)
            return cr, ci

        cr, ci = lax.fori_loop(0, rows // (SUBLANES * S5_UNROLL), block, (cr_s[:, cs], ci_s[:, cs]))
        cr_s[:, cs] = cr
        ci_s[:, cs] = ci

    ys = []
    for q in range(S5_DIAG):
        ns = slice(q * nq, (q + 1) * nq)
        ys.append(jnp.dot(xr_s[:, ns].astype(BF16), w2r_ref[q], preferred_element_type=F32)
                  - jnp.dot(xi_s[:, ns].astype(BF16), w2i_ref[q], preferred_element_type=F32))
    y = jnp.concatenate(ys, axis=1)
    y = _gelu(y + d_ref[...] * u)
    z = jnp.dot(y.astype(BF16), wg_ref[...], preferred_element_type=F32) + bg_ref[...]
    o_ref[0] = y * jax.nn.sigmoid(z)


def _s5(proj, col_block, a_re, a_im, log_dt, b_re, b_im, c_re, c_im, d_skip, w_glu, b_glu, tb):
    bsz, seq, _ = proj.shape
    g, p = a_re.shape
    hh = b_re.shape[2]
    w = g * hh
    n = g * p
    pre, pim, bbre, bbim = _s5prep(a_re, a_im, log_dt, b_re, b_im)
    gq = g // S5_DIAG
    eye = jnp.eye(gq, dtype=F32)

    def blockdiag(m):
        a, b = m.shape[1], m.shape[2]
        m = m.reshape(S5_DIAG, gq, a, b)
        return (eye[None, :, None, :, None] * m[:, :, :, None, :]).reshape(S5_DIAG, gq * a, gq * b)

    w1 = jnp.concatenate([blockdiag(bbre), blockdiag(bbim)], axis=2).astype(BF16)
    w2r = blockdiag(jnp.swapaxes(c_re, 1, 2)).astype(BF16)
    w2i = blockdiag(jnp.swapaxes(c_im, 1, 2)).astype(BF16)
    pre = pre.reshape(SUBLANES, n)
    pim = pim.reshape(SUBLANES, n)
    rid = jnp.arange(SUBLANES)[:, None]

    def step_table(d):
        return (jnp.where(rid >= d, pre[d - 1][None, :], 0.0), jnp.where(rid >= d, pim[d - 1][None, :], 0.0))

    t1r, t1i = step_table(1)
    t2r, t2i = step_table(2)
    t4r, t4i = step_table(4)

    def full(a):
        return pl.BlockSpec(a.shape, lambda b, i: (0,) * a.ndim)

    consts = [w1, t1r, t1i, t2r, t2i, t4r, t4i, pre, pim, w2r, w2i,
              d_skip.reshape(1, w), w_glu.astype(BF16), b_glu.reshape(1, w)]
    return pl.pallas_call(
        _s5_kernel,
        grid=(bsz, seq // tb),
        in_specs=[pl.BlockSpec((1, tb, w), lambda b, i: (b, i, col_block))] + [full(a) for a in consts],
        out_specs=pl.BlockSpec((1, tb, w), lambda b, i: (b, i, 0)),
        out_shape=jax.ShapeDtypeStruct((bsz, seq, w), F32),
        scratch_shapes=[pltpu.VMEM((tb, n), F32), pltpu.VMEM((tb, n), F32),
                        pltpu.VMEM((SUBLANES, n), F32), pltpu.VMEM((SUBLANES, n), F32)],
        compiler_params=_cparams("arbitrary", "arbitrary", vmem=VMEM_LIMIT),
        name="s5",
    )(proj, *consts)


def _mixout_kernel(ohg_ref, os5_ref, x_ref, mod_ref, wo1_ref, wo2_ref, nw_ref, wq_ref, k1_ref, k2_ref,
                   x1_ref, h2_ref, sc_ref):
    mixed = (jnp.dot(ohg_ref[0].astype(BF16), wo1_ref[...], preferred_element_type=F32)
             + jnp.dot(os5_ref[0].astype(BF16), wo2_ref[...], preferred_element_type=F32))
    x1 = x_ref[0] + mod_ref[0, 2:3, :] * mixed
    x1_ref[0] = x1
    h2 = _norm_mod(x1, nw_ref[...], mod_ref[0, 4:5, :], mod_ref[0, 3:4, :])
    for j in range(SUBLANES):
        h2_ref[0, :, j, :] = h2[:, j * LANES:(j + 1) * LANES]
    q = jnp.dot(h2.astype(BF16), wq_ref[...], preferred_element_type=F32)
    for m in range(2 * PEER_HEADS):
        keys = k1_ref[...] if m % 2 == 0 else k2_ref[...]
        qh = q[:, m * PEER_HALF:(m + 1) * PEER_HALF].astype(BF16)
        sc_ref[0, m] = lax.dot_general(keys, qh, (((1,), (1,)), ((), ())), preferred_element_type=F32)


def _mixout(o_hg, o_s5, x, mod6, w_out, nw, w_q, keys1, keys2, tm):
    bsz, seq, d = x.shape
    w = o_hg.shape[2]
    nq = w_q.shape[1]
    wo = w_out.astype(BF16)

    def full(a):
        return pl.BlockSpec(a.shape, lambda b, i: (0,) * a.ndim)

    consts_a = [wo[:w], wo[w:]]
    consts_b = [nw.reshape(1, d), w_q.astype(BF16), keys1.astype(BF16), keys2.astype(BF16)]
    return pl.pallas_call(
        _mixout_kernel,
        grid=(bsz, seq // tm),
        in_specs=[pl.BlockSpec((1, tm, w), lambda b, i: (b, i, 0)),
                  pl.BlockSpec((1, tm, w), lambda b, i: (b, i, 0)),
                  pl.BlockSpec((1, tm, d), lambda b, i: (b, i, 0)),
                  pl.BlockSpec((1, 6, d), lambda b, i: (b, 0, 0))]
                 + [full(a) for a in consts_a] + [full(a) for a in consts_b],
        out_specs=[pl.BlockSpec((1, tm, d), lambda b, i: (b, i, 0)),
                   pl.BlockSpec((1, tm, SUBLANES, LANES), lambda b, i: (b, i, 0, 0)),
                   pl.BlockSpec((1, 2 * PEER_HEADS, PEER_NKEYS, tm), lambda b, i: (b, 0, 0, i))],
        out_shape=(jax.ShapeDtypeStruct((bsz, seq, d), F32),
                   jax.ShapeDtypeStruct((bsz, seq, SUBLANES, LANES), F32),
                   jax.ShapeDtypeStruct((bsz, 2 * PEER_HEADS, PEER_NKEYS, seq), F32)),
        compiler_params=_cparams("arbitrary", "arbitrary", vmem=VMEM_LIMIT),
        name="mixout",
    )(o_hg, o_s5, x, mod6, *consts_a, *consts_b)


def _extract_max(tiles, ids, *payloads):
    lvl = list(zip(tiles, ids, *payloads))
    while len(lvl) > 1:
        nxt = []
        for a, b in zip(lvl[0::2], lvl[1::2]):
            take = a[0] >= b[0]
            nxt.append(tuple(jnp.where(take, x, y) for x, y in zip(a, b)))
        if len(lvl) % 2:
            nxt.append(lvl[-1])
        lvl = nxt
    best, best_id = lvl[0][0], lvl[0][1]
    m = jnp.max(best, axis=0, keepdims=True)
    big = jnp.iinfo(jnp.int32).max
    am = jnp.min(jnp.where(best == m, best_id, big), axis=0, keepdims=True)
    picked = [jnp.max(jnp.where(best_id == am, p, -1), axis=0, keepdims=True) for p in lvl[0][2:]]
    return (m, am, *picked)


def _topk_rows(s, k):
    rows, tt = s.shape
    row = lax.broadcasted_iota(I32, (SUBLANES, tt), 0)
    tiles = [s[j:j + SUBLANES] for j in range(0, rows, SUBLANES)]
    ids = [row + j for j in range(0, rows, SUBLANES)]
    vals, idxs = [], []
    for it in range(k):
        m, am = _extract_max(tiles, ids)
        vals.append(m)
        idxs.append(am)
        if it + 1 < k:
            tiles = [jnp.where(i == am, -jnp.inf, t) for t, i in zip(tiles, ids)]
    return vals, idxs


def _topk_kernel(sc_ref, e_ref, g_ref):
    k = PEER_TOPK
    tt = sc_ref.shape[3]
    assert k == 2 * SUBLANES

    def head(h, carry):
        v1, i1 = _topk_rows(sc_ref[0, 2 * h], k)
        v2, i2 = _topk_rows(sc_ref[0, 2 * h + 1], k)
        v1a = jnp.concatenate(v1, axis=0)
        i1a = jnp.concatenate(i1, axis=0)
        v2a = jnp.concatenate(v2, axis=0)
        i2a = jnp.concatenate(i2, axis=0)
        row = lax.broadcasted_iota(I32, (SUBLANES, tt), 0)
        cands, cidxs, cposs = [], [], []
        tail = k // 2
        for a in range(tail):
            nb = k // (a + 1)
            for b0 in range(0, nb, SUBLANES):
                c = v1[a] + v2a[b0:b0 + SUBLANES]
                if nb - b0 < SUBLANES:
                    c = jnp.where(row < nb - b0, c, -jnp.inf)
                cands.append(c)
                cidxs.append(i1[a] * PEER_NKEYS + i2a[b0:b0 + SUBLANES])
                cposs.append(a * k + b0 + row)
        cands.append(v1a[tail:] + v2[0])
        cidxs.append(i1a[tail:] * PEER_NKEYS + i2[0])
        cposs.append((tail + row) * k)

        best, eidx = [], []
        for it in range(k):
            m, pos, e = _extract_max(cands, cposs, cidxs)
            best.append(m)
            eidx.append(e)
            if it + 1 < k:
                cands = [jnp.where(cp == pos, -jnp.inf, c) for cp, c in zip(cposs, cands)]
        best = jnp.concatenate(best, axis=0)
        ex = jnp.exp(best - best[0:1, :])
        r0 = pl.multiple_of(h * k, k)
        g_ref[pl.ds(r0, k), :] = ex / jnp.sum(ex, axis=0, keepdims=True)
        e_ref[pl.ds(r0, k), :] = jnp.concatenate(eidx, axis=0)
        return carry

    lax.fori_loop(0, PEER_HEADS, head, 0)


def _topk(scores_t, tt):
    bsz, m, nk, seq = scores_t.shape
    rows = PEER_HEADS * PEER_TOPK
    nblk = seq // tt
    out_blk = pl.BlockSpec((rows, tt), lambda b, i: (0, b * nblk + i))
    return pl.pallas_call(
        _topk_kernel,
        grid=(bsz, nblk),
        in_specs=[pl.BlockSpec((1, m, nk, tt), lambda b, i: (b, 0, 0, i))],
        out_specs=[out_blk, out_blk],
        out_shape=(jax.ShapeDtypeStruct((rows, bsz * seq), I32),
                   jax.ShapeDtypeStruct((rows, bsz * seq), F32)),
        compiler_params=_cparams("arbitrary", "arbitrary", vmem=VMEM_LIMIT),
        name="topk",
    )(scores_t)


def _table_tiles(tab):
    e, d = tab.shape
    return tab.astype(BF16).reshape(e, d // LANES, LANES)


def _sublane_sums(p):
    row = lax.broadcasted_iota(I32, (SUBLANES, LANES), 0)
    m4 = row < 4
    m2 = (row & 3) < 2
    m1 = (row & 1) < 1

    def fold4(a, b):
        return jnp.where(m4, a, b) + pltpu.roll(jnp.where(m4, b, a), 4, 0)

    def fold(c1, c2, m, d):
        return (jnp.where(m, c1, pltpu.roll(c2, d, 0))
                + jnp.where(m, pltpu.roll(c1, SUBLANES - d, 0), c2))

    d1 = fold(fold4(p[0], p[4]), fold4(p[2], p[6]), m2, 2)
    d2 = fold(fold4(p[1], p[5]), fold4(p[3], p[7]), m2, 2)
    return fold(d1, d2, m1, 1)


def _stage_ids(idx_hbm, id_refs, sems, tb):
    i = pl.program_id(0)

    def copies(blk, buf):
        return [pltpu.make_async_copy(idx_hbm.at[k, pl.ds(blk * tb, tb)], ref.at[pl.ds(buf * tb, tb)], sems.at[buf])
                for k, ref in enumerate(id_refs)]

    @pl.when(i == 0)
    def _():
        for cp in copies(0, 0):
            cp.start()

    @pl.when(i + 1 < pl.num_programs(0))
    def _():
        for cp in copies(i + 1, (i + 1) % 2):
            cp.start()

    for cp in copies(i, i % 2):
        cp.wait()
    return (i % 2) * tb


def _peer_u_kernel(idx_hbm, h_ref, g_ref, tab_ref, o_ref, acts_s, red_s, sems, *id_refs):
    nk, tb = g_ref.shape
    groups = nk // SUBLANES
    lane = lax.broadcasted_iota(I32, (SUBLANES, tb), 1)
    id0 = _stage_ids(idx_hbm, id_refs, sems, tb)
    acts_s[...] = jnp.zeros_like(acts_s)
    red_s[...] = jnp.zeros_like(red_s)

    def finish(t):
        for g in range(groups):
            rows = slice(g * SUBLANES, (g + 1) * SUBLANES)
            col = jnp.sum(red_s[g], axis=1, keepdims=True)
            acts_s[rows, :] = jnp.where(lane == t, col, acts_s[rows, :])

    def token(t, carry):
        finish(t - 1)
        h = h_ref[t]
        ti = id0 + t
        for g in range(groups):
            p = [tab_ref[id_refs[g * SUBLANES + i][ti]].astype(F32) * h for i in range(SUBLANES)]
            red_s[g] = _sublane_sums(p)
        return carry

    lax.fori_loop(0, tb, token, 0)
    finish(tb - 1)
    o_ref[...] = g_ref[...] * _gelu(acts_s[...])


def _peer_u(eidx, h2r, gates, tab, tb):
    nk, nt = gates.shape
    return pl.pallas_call(
        _peer_u_kernel,
        grid=(nt // tb,),
        in_specs=[pl.BlockSpec(memory_space=pl.ANY),
                  pl.BlockSpec((tb, SUBLANES, LANES), lambda i: (i, 0, 0)),
                  pl.BlockSpec((nk, tb), lambda i: (0, i)),
                  pl.BlockSpec(tab.shape, lambda i: (0, 0, 0), pipeline_mode=pl.Buffered(1))],
        out_specs=pl.BlockSpec((nk, tb), lambda i: (0, i)),
        out_shape=jax.ShapeDtypeStruct((nk, nt), F32),
        scratch_shapes=[pltpu.VMEM((nk, tb), F32), pltpu.VMEM((nk // SUBLANES, SUBLANES, LANES), F32),
                        pltpu.SemaphoreType.DMA((2,))] + [pltpu.SMEM((2 * tb,), I32)] * nk,
        compiler_params=_cparams("arbitrary", vmem=VMEM_LIMIT),
        name="peer_u",
    )(eidx, h2r, gates, tab)


def _peer_v_kernel(idx_hbm, c_ref, tab_ref, o_ref, cb0_s, cb1_s, part_s, sems, *id_refs):
    nk, tb = c_ref.shape
    groups = nk // SUBLANES
    lane = lax.broadcasted_iota(I32, (SUBLANES, tb), 1)
    id0 = _stage_ids(idx_hbm, id_refs, sems, tb)

    def prep(t, cb):
        for g in range(groups):
            rows = slice(g * SUBLANES, (g + 1) * SUBLANES)
            col = jnp.sum(jnp.where(lane == t, c_ref[rows, :], 0.0), axis=1, keepdims=True)
            cb[rows, :] = jnp.broadcast_to(col, (SUBLANES, LANES))

    def gather(t, cb):
        ti = id0 + t
        for g in range(groups):
            p = [cb[k:k + 1, :] * tab_ref[id_refs[k][ti]].astype(F32)
                 for k in range(g * SUBLANES, (g + 1) * SUBLANES)]
            part_s[g] = ((p[0] + p[1]) + (p[2] + p[3])) + ((p[4] + p[5]) + (p[6] + p[7]))

    def finish(t):
        s = [part_s[g] for g in range(groups)]
        while len(s) > 1:
            s = [s[i] + s[i + 1] for i in range(0, len(s), 2)]
        o_ref[t] = s[0]

    prep(0, cb0_s)
    part_s[...] = jnp.zeros_like(part_s)

    def token(t, carry):
        prep(t + 1, cb1_s)
        finish(jnp.maximum(t - 1, 0))
        gather(t, cb0_s)
        cb0_s[...] = cb1_s[...]
        return carry

    lax.fori_loop(0, tb, token, 0)
    finish(tb - 1)


def _peer_v(eidx, coef, tab, tb):
    nk, nt = coef.shape
    return pl.pallas_call(
        _peer_v_kernel,
        grid=(nt // tb,),
        in_specs=[pl.BlockSpec(memory_space=pl.ANY),
                  pl.BlockSpec((nk, tb), lambda i: (0, i)),
                  pl.BlockSpec(tab.shape, lambda i: (0, 0, 0), pipeline_mode=pl.Buffered(1))],
        out_specs=pl.BlockSpec((tb, SUBLANES, LANES), lambda i: (i, 0, 0)),
        out_shape=jax.ShapeDtypeStruct((nt, SUBLANES, LANES), F32),
        scratch_shapes=[pltpu.VMEM((nk, LANES), F32), pltpu.VMEM((nk, LANES), F32),
                        pltpu.VMEM((nk // SUBLANES, SUBLANES, LANES), F32),
                        pltpu.SemaphoreType.DMA((2,))] + [pltpu.SMEM((2 * tb,), I32)] * nk,
        compiler_params=_cparams("arbitrary", vmem=VMEM_LIMIT),
        name="peer_v",
    )(eidx, coef, tab)


def _final_kernel(last, x_ref, p_ref, mod_ref, nw_ref, o_ref):
    peer = jnp.concatenate([p_ref[0, :, j, :] for j in range(SUBLANES)], axis=1)
    x2 = x_ref[0] + mod_ref[0, 5:6, :] * peer
    if last:
        ms = jnp.mean(x2 * x2, axis=-1, keepdims=True)
        x2 = x2 * lax.rsqrt(ms + EPS) * nw_ref[...]
    o_ref[0] = x2


def _final(x1, peer_tiles, mod6, nw, last, tm):
    bsz, seq, d = x1.shape
    blk = pl.BlockSpec((1, tm, d), lambda b, i: (b, i, 0))
    return pl.pallas_call(
        functools.partial(_final_kernel, last),
        grid=(bsz, seq // tm),
        in_specs=[blk, pl.BlockSpec((1, tm, SUBLANES, LANES), lambda b, i: (b, i, 0, 0)),
                  pl.BlockSpec((1, 6, d), lambda b, i: (b, 0, 0)),
                  pl.BlockSpec((1, d), lambda b, i: (0, 0))],
        out_specs=blk,
        out_shape=jax.ShapeDtypeStruct((bsz, seq, d), F32),
        compiler_params=_cparams("arbitrary", "arbitrary", vmem=VMEM_LIMIT),
        name="final",
    )(x1, peer_tiles, mod6, nw.reshape(1, d))


def _tile(n, pref):
    t = min(n, pref)
    assert n % t == 0, (n, t)
    return t


def _token_tiles(seq, nt):
    return {
        "inproj": _tile(seq, 512),
        "hgrn2": _tile(seq, 512),
        "s5": _tile(seq, 256),
        "mixout": _tile(seq, 512),
        "topk": _tile(seq, 512),
        "peer": _tile(nt, 128),
        "final": _tile(seq, 512),
    }


def kernel(x, c, ada_w, ada_b, norm_mix_w, norm_ffn_w, w_in, w_out, hg_lower_bounds, hg_gnorm_w,
           s5_a_re, s5_a_im, s5_log_dt, s5_b_re, s5_b_im, s5_c_re, s5_c_im, s5_d, s5_glu_w, s5_glu_b,
           peer_wq, peer_keys1, peer_keys2, peer_u, peer_v, final_norm_w):
    bsz, seq, d = x.shape
    depth = ada_w.shape[0]
    nt = bsz * seq
    hg_w = HG_HEADS * HG_DK
    assert d == SUBLANES * LANES and seq % HG_CHUNK == 0
    assert w_in.shape[2] == 4 * hg_w + s5_a_re.shape[1] * S5_GROUP and 4 * hg_w % (s5_a_re.shape[1] * S5_GROUP) == 0
    s5_col = 4 * hg_w // (s5_a_re.shape[1] * S5_GROUP)

    tiles = _token_tiles(seq, nt)
    for l in range(depth):
        mod6 = _mod(c, ada_w[l], ada_b[l]).reshape(bsz, 6, d)
        proj = _inproj(x, mod6, norm_mix_w[l], w_in[l].astype(BF16), tiles["inproj"])
        o_hg = _hgrn2(proj, hg_lower_bounds, hg_gnorm_w[l], l, tiles["hgrn2"])
        o_s5 = _s5(proj, s5_col, s5_a_re[l], s5_a_im[l], s5_log_dt[l], s5_b_re[l], s5_b_im[l],
                   s5_c_re[l], s5_c_im[l], s5_d[l], s5_glu_w[l], s5_glu_b[l], tiles["s5"])
        x1, h2_tiles, scores_t = _mixout(o_hg, o_s5, x, mod6, w_out[l], norm_ffn_w[l], peer_wq[l],
                                         peer_keys1[l], peer_keys2[l], tiles["mixout"])
        eidx, gates = _topk(scores_t, tiles["topk"])
        coef = _peer_u(eidx, h2_tiles.reshape(nt, SUBLANES, LANES), gates, _table_tiles(peer_u[l]), tiles["peer"])
        peer_tiles = _peer_v(eidx, coef, _table_tiles(peer_v[l]), tiles["peer"])
        x = _final(x1, peer_tiles.reshape(bsz, seq, SUBLANES, LANES), mod6, final_norm_w, l == depth - 1,
                   tiles["final"])
    return x
```

```python
import functools
import math

import jax
import jax.numpy as jnp
from jax import lax
from jax.experimental import pallas as pl
from jax.experimental.pallas import tpu as pltpu

F32 = jnp.float32
BF16 = jnp.bfloat16
I32 = jnp.int32

EPS = 1e-6
HG_HEADS = 4
HG_DK = 128
HG_CHUNK = 64
HG_CHUNKS_PER_ITER = 4
HG_MAX_FACTORED_DECAY = 60.0
S5_GROUP = 16
S5_STATE = 64
PEER_HEADS = 8
PEER_NKEYS = 128
PEER_TOPK = 16
PEER_HALF = 128

LANES = 128
SUBLANES = 8
VMEM_LIMIT = 56 * 1024 * 1024

_HIGHEST = lax.Precision.HIGHEST
_SQRT_HALF = 0.7071067811865476


def _cparams(*sem, vmem=None):
    return pltpu.CompilerParams(dimension_semantics=sem, vmem_limit_bytes=vmem)


def _silu(x):
    return x * jax.nn.sigmoid(x)


def _gelu(x):
    return 0.5 * x * (1.0 + lax.erf(x * _SQRT_HALF))


def _bdot(a, b):
    return jnp.dot(a.astype(BF16), b.astype(BF16), preferred_element_type=F32)


def _mod_kernel(c_ref, w_ref, b_ref, o_ref):
    cond = _silu(c_ref[...])
    o_ref[...] = jnp.dot(cond, w_ref[...], preferred_element_type=F32, precision=_HIGHEST) + b_ref[...]


def _mod(c, w, b):
    bsz, d = c.shape
    n = w.shape[1]
    return pl.pallas_call(
        _mod_kernel,
        grid=(n // d,),
        in_specs=[pl.BlockSpec((bsz, d), lambda j: (0, 0)),
                  pl.BlockSpec((d, d), lambda j: (0, j)),
                  pl.BlockSpec((1, d), lambda j: (0, j))],
        out_specs=pl.BlockSpec((bsz, d), lambda j: (0, j)),
        out_shape=jax.ShapeDtypeStruct((bsz, n), F32),
        compiler_params=_cparams("arbitrary"),
        name="mod",
    )(c, w, b.reshape(1, n))


def _norm_mod(x, nw, scale, shift):
    ms = jnp.mean(x * x, axis=-1, keepdims=True)
    return (x * lax.rsqrt(ms + EPS) * nw) * (1.0 + scale) + shift


def _inproj_kernel(x_ref, mod_ref, nw_ref, w_ref, o_ref):
    h = _norm_mod(x_ref[0], nw_ref[...], mod_ref[0, 1:2, :], mod_ref[0, 0:1, :])
    o_ref[0] = jnp.dot(h.astype(BF16), w_ref[...], preferred_element_type=F32)


def _inproj(x, mod6, nw, w_bf16, tm):
    bsz, seq, d = x.shape
    n = w_bf16.shape[1]
    return pl.pallas_call(
        _inproj_kernel,
        grid=(bsz, seq // tm),
        in_specs=[pl.BlockSpec((1, tm, d), lambda b, i: (b, i, 0)),
                  pl.BlockSpec((1, 6, d), lambda b, i: (b, 0, 0)),
                  pl.BlockSpec((1, d), lambda b, i: (0, 0)),
                  pl.BlockSpec((d, n), lambda b, i: (0, 0))],
        out_specs=pl.BlockSpec((1, tm, n), lambda b, i: (b, i, 0)),
        out_shape=jax.ShapeDtypeStruct((bsz, seq, n), F32),
        compiler_params=_cparams("arbitrary", "arbitrary", vmem=VMEM_LIMIT),
        name="inproj",
    )(x, mod6, nw.reshape(1, d), w_bf16)


def _hgrn2_kernel(layer, q_ref, f_ref, i_ref, g_ref, lbp_ref, gnw_ref, o_ref,
                  st_ref, b_s, qs_s, oi_s):
    c_len = HG_CHUNK
    n_chunks = q_ref.shape[1] // c_len

    @pl.when(pl.program_id(1) == 0)
    def _():
        st_ref[...] = jnp.zeros_like(st_ref)

    lbp = lbp_ref[...]
    e = jnp.exp(lbp - jnp.max(lbp, axis=0, keepdims=True))
    lb_all = jnp.sum(e[: layer + 1], axis=0, keepdims=True) / jnp.sum(e, axis=0, keepdims=True)
    gnw = gnw_ref[...]
    row = lax.broadcasted_iota(I32, (c_len, c_len), 0)
    col = lax.broadcasted_iota(I32, (c_len, c_len), 1)
    tril = (col <= row).astype(F32)
    s_iota = lax.broadcasted_iota(I32, (c_len, 1), 0)

    def chunk_group(gi, carry):
        units = []
        worst = None
        for c in range(HG_CHUNKS_PER_ITER):
            r0 = pl.multiple_of((gi * HG_CHUNKS_PER_ITER + c) * c_len, c_len)
            for h in range(HG_HEADS):
                cs = slice(h * HG_DK, (h + 1) * HG_DK)
                lb = lb_all[:, cs]
                f = lb + (1.0 - lb) * jax.nn.sigmoid(f_ref[0, pl.ds(r0, c_len), cs])
                kk = 1.0 - f
                qs = _silu(q_ref[0, pl.ds(r0, c_len), cs]) * (HG_DK ** -0.5)
                v = i_ref[0, pl.ds(r0, c_len), cs]
                b = jnp.dot(tril, jnp.log(f), preferred_element_type=F32, precision=_HIGHEST)
                units.append((r0, h, cs, kk, qs, v, b))
                decay = jnp.max(-b[c_len - 1:c_len, :])
                worst = decay if worst is None else jnp.maximum(worst, decay)

        def finish(unit, o_intra):
            r0, h, cs, kk, qs, v, b = unit
            st = st_ref[h]
            o_inter = lax.dot_general((qs * jnp.exp(b)).astype(BF16), st.astype(BF16),
                                      (((1,), (1,)), ((), ())), preferred_element_type=F32)
            o = o_inter + o_intra
            b_last = b[c_len - 1:c_len, :]
            kdec = kk * jnp.exp(b_last - b)
            upd = lax.dot_general(v.astype(BF16), kdec.astype(BF16),
                                  (((0,), (0,)), ((), ())), preferred_element_type=F32)
            st_ref[h] = st * jnp.exp(b_last) + upd
            y = o * lax.rsqrt(jnp.mean(o * o, axis=-1, keepdims=True) + EPS) * gnw
            o_ref[0, pl.ds(r0, c_len), cs] = y * _silu(g_ref[0, pl.ds(r0, c_len), cs])

        factorable = worst < HG_MAX_FACTORED_DECAY

        @pl.when(factorable)
        def _():
            for unit in units:
                _, _, _, kk, qs, v, b = unit
                qt = (qs * jnp.exp(b)).astype(BF16)
                kt = (kk * jnp.exp(-b)).astype(BF16)
                sc = lax.dot_general(qt, kt, (((1,), (1,)), ((), ())), preferred_element_type=F32)
                finish(unit, jnp.dot((sc * tril).astype(BF16), v.astype(BF16), preferred_element_type=F32))

        @pl.when(jnp.logical_not(factorable))
        def _():
            for unit in units:
                _, _, _, kk, qs, v, b = unit
                b_s[...] = b
                qs_s[...] = qs

                def row_step(t, c2, kk=kk, v=v, b=b):
                    bt = b_s[pl.ds(t, 1), :]
                    qt = qs_s[pl.ds(t, 1), :]
                    w = (qt * kk) * jnp.exp(jnp.minimum(bt - b, 0.0))
                    sc = jnp.sum(w, axis=1, keepdims=True)
                    sc = jnp.where(s_iota <= t, sc, 0.0)
                    oi_s[pl.ds(t, 1), :] = jnp.sum(sc * v, axis=0, keepdims=True)
                    return c2

                lax.fori_loop(0, c_len, row_step, 0)
                finish(unit, oi_s[...])

        return carry

    assert n_chunks % HG_CHUNKS_PER_ITER == 0
    lax.fori_loop(0, n_chunks // HG_CHUNKS_PER_ITER, chunk_group, 0)


def _hgrn2(proj, lbp, gnw, layer, tb):
    bsz, seq, _ = proj.shape
    w = HG_HEADS * HG_DK
    nl = lbp.shape[0]

    def col_spec(j):
        return pl.BlockSpec((1, tb, w), lambda b, i, j=j: (b, i, j))

    return pl.pallas_call(
        functools.partial(_hgrn2_kernel, layer),
        grid=(bsz, seq // tb),
        in_specs=[col_spec(0), col_spec(1), col_spec(2), col_spec(3),
                  pl.BlockSpec((nl, w), lambda b, i: (0, 0)),
                  pl.BlockSpec((1, HG_DK), lambda b, i: (0, 0))],
        out_specs=pl.BlockSpec((1, tb, w), lambda b, i: (b, i, 0)),
        out_shape=jax.ShapeDtypeStruct((bsz, seq, w), F32),
        scratch_shapes=[pltpu.VMEM((HG_HEADS, HG_DK, HG_DK), F32),
                        pltpu.VMEM((HG_CHUNK, HG_DK), F32),
                        pltpu.VMEM((HG_CHUNK, HG_DK), F32),
                        pltpu.VMEM((HG_CHUNK, HG_DK), F32)],
        compiler_params=_cparams("arbitrary", "arbitrary", vmem=VMEM_LIMIT),
        name="hgrn2",
    )(proj, proj, proj, proj, lbp, gnw.reshape(1, HG_DK))


def _s5prep_kernel(are_ref, aim_ref, ldt_ref, bre_ref, bim_ref, pre_ref, pim_ref, bbre_ref, bbim_ref):
    ar = are_ref[...]
    ai = aim_ref[...]
    dt = jnp.exp(ldt_ref[...])
    mag = jnp.exp(ar * dt)
    abr = mag * jnp.cos(ai * dt)
    abi = mag * jnp.sin(ai * dt)
    den = ar * ar + ai * ai
    nr = abr - 1.0
    ni = abi
    cre = (nr * ar + ni * ai) / den
    cim = (ni * ar - nr * ai) / den
    bre = bre_ref[...]
    bim = bim_ref[...]
    bbre_ref[...] = cre[:, None, :] * bre - cim[:, None, :] * bim
    bbim_ref[...] = cre[:, None, :] * bim + cim[:, None, :] * bre
    pr, pi = abr, abi
    pre_ref[0] = pr
    pim_ref[0] = pi
    for k in range(1, SUBLANES):
        pr, pi = pr * abr - pi * abi, pr * abi + pi * abr
        pre_ref[k] = pr
        pim_ref[k] = pi


def _s5prep(a_re, a_im, log_dt, b_re, b_im):
    g, p = a_re.shape
    hh = b_re.shape[2]
    out = pl.pallas_call(
        _s5prep_kernel,
        out_shape=(jax.ShapeDtypeStruct((SUBLANES, g, p), F32), jax.ShapeDtypeStruct((SUBLANES, g, p), F32),
                   jax.ShapeDtypeStruct((g, hh, p), F32), jax.ShapeDtypeStruct((g, hh, p), F32)),
        name="s5prep",
    )(a_re, a_im, log_dt.reshape(g, 1), jnp.swapaxes(b_re, 1, 2), jnp.swapaxes(b_im, 1, 2))
    return out


S5_COLS = 256
S5_UNROLL = 4
S5_DIAG = 2


def _s5_kernel(u_ref, w1_ref, t1r, t1i, t2r, t2i, t4r, t4i, pr_ref, pi_ref, w2r_ref, w2i_ref,
               d_ref, wg_ref, bg_ref, o_ref, xr_s, xi_s, cr_s, ci_s):
    n = xr_s.shape[1]
    rows = xr_s.shape[0]

    @pl.when(pl.program_id(1) == 0)
    def _():
        cr_s[...] = jnp.zeros_like(cr_s)
        ci_s[...] = jnp.zeros_like(ci_s)

    u = u_ref[0]
    ub = u.astype(BF16)
    w = u.shape[1]
    wq, nq = w // S5_DIAG, n // S5_DIAG
    for q in range(S5_DIAG):
        bu = jnp.dot(ub[:, q * wq:(q + 1) * wq], w1_ref[q], preferred_element_type=F32)
        xr_s[:, q * nq:(q + 1) * nq] = bu[:, :nq]
        xi_s[:, q * nq:(q + 1) * nq] = bu[:, nq:]

    for c0 in range(0, n, S5_COLS):
        cs = slice(c0, c0 + S5_COLS)
        steps = ((1, t1r[:, cs], t1i[:, cs]), (2, t2r[:, cs], t2i[:, cs]), (4, t4r[:, cs], t4i[:, cs]))
        pr = pr_ref[:, cs]
        pi = pi_ref[:, cs]

        def block(j, carry, cs=cs, steps=steps, pr=pr, pi=pi):
            cr, ci = carry
            local = []
            for q in range(S5_UNROLL):
                r0 = pl.multiple_of((j * S5_UNROLL + q) * SUBLANES, SUBLANES)
                xr = xr_s[pl.ds(r0, SUBLANES), cs]
                xi = xi_s[pl.ds(r0, SUBLANES), cs]
                for d, ar, ai in steps:
                    sr = pltpu.roll(xr, d, 0)
                    si = pltpu.roll(xi, d, 0)
                    xr, xi = xr + ar * sr - ai * si, xi + ar * si + ai * sr
                local.append((r0, xr, xi))
            last = SUBLANES - 1
            for r0, xr, xi in local:
                xr, xi = xr + pr * cr - pi * ci, xi + pr * ci + pi * cr
                xr_s[pl.ds(r0, SUBLANES), cs] = xr
                xi_s[pl.ds(r0, SUBLANES), cs] = xi
                cr = jnp.broadcast_to(xr[last:, :], xr.shape)
                ci = jnp.broadcast_to(xi[last:, :], xi.shape)
            return cr, ci

        cr, ci = lax.fori_loop(0, rows // (SUBLANES * S5_UNROLL), block, (cr_s[:, cs], ci_s[:, cs]))
        cr_s[:, cs] = cr
        ci_s[:, cs] = ci

    ys = []
    for q in range(S5_DIAG):
        ns = slice(q * nq, (q + 1) * nq)
        ys.append(jnp.dot(xr_s[:, ns].astype(BF16), w2r_ref[q], preferred_element_type=F32)
                  - jnp.dot(xi_s[:, ns].astype(BF16), w2i_ref[q], preferred_element_type=F32))
    y = jnp.concatenate(ys, axis=1)
    y = _gelu(y + d_ref[...] * u)
    z = jnp.dot(y.astype(BF16), wg_ref[...], preferred_element_type=F32) + bg_ref[...]
    o_ref[0] = y * jax.nn.sigmoid(z)


def _s5(proj, col_block, a_re, a_im, log_dt, b_re, b_im, c_re, c_im, d_skip, w_glu, b_glu, tb):
    bsz, seq, _ = proj.shape
    g, p = a_re.shape
    hh = b_re.shape[2]
    w = g * hh
    n = g * p
    pre, pim, bbre, bbim = _s5prep(a_re, a_im, log_dt, b_re, b_im)
    gq = g // S5_DIAG
    eye = jnp.eye(gq, dtype=F32)

    def blockdiag(m):
        a, b = m.shape[1], m.shape[2]
        m = m.reshape(S5_DIAG, gq, a, b)
        return (eye[None, :, None, :, None] * m[:, :, :, None, :]).reshape(S5_DIAG, gq * a, gq * b)

    w1 = jnp.concatenate([blockdiag(bbre), blockdiag(bbim)], axis=2).astype(BF16)
    w2r = blockdiag(jnp.swapaxes(c_re, 1, 2)).astype(BF16)
    w2i = blockdiag(jnp.swapaxes(c_im, 1, 2)).astype(BF16)
    pre = pre.reshape(SUBLANES, n)
    pim = pim.reshape(SUBLANES, n)
    rid = jnp.arange(SUBLANES)[:, None]

    def step_table(d):
        return (jnp.where(rid >= d, pre[d - 1][None, :], 0.0), jnp.where(rid >= d, pim[d - 1][None, :], 0.0))

    t1r, t1i = step_table(1)
    t2r, t2i = step_table(2)
    t4r, t4i = step_table(4)

    def full(a):
        return pl.BlockSpec(a.shape, lambda b, i: (0,) * a.ndim)

    consts = [w1, t1r, t1i, t2r, t2i, t4r, t4i, pre, pim, w2r, w2i,
              d_skip.reshape(1, w), w_glu.astype(BF16), b_glu.reshape(1, w)]
    return pl.pallas_call(
        _s5_kernel,
        grid=(bsz, seq // tb),
        in_specs=[pl.BlockSpec((1, tb, w), lambda b, i: (b, i, col_block))] + [full(a) for a in consts],
        out_specs=pl.BlockSpec((1, tb, w), lambda b, i: (b, i, 0)),
        out_shape=jax.ShapeDtypeStruct((bsz, seq, w), F32),
        scratch_shapes=[pltpu.VMEM((tb, n), F32), pltpu.VMEM((tb, n), F32),
                        pltpu.VMEM((SUBLANES, n), F32), pltpu.VMEM((SUBLANES, n), F32)],
        compiler_params=_cparams("arbitrary", "arbitrary", vmem=VMEM_LIMIT),
        name="s5",
    )(proj, *consts)


def _mixout_kernel(ohg_ref, os5_ref, x_ref, mod_ref, wo1_ref, wo2_ref, nw_ref, wq_ref, k1_ref, k2_ref,
                   x1_ref, h2_ref, sc_ref):
    mixed = (jnp.dot(ohg_ref[0].astype(BF16), wo1_ref[...], preferred_element_type=F32)
             + jnp.dot(os5_ref[0].astype(BF16), wo2_ref[...], preferred_element_type=F32))
    x1 = x_ref[0] + mod_ref[0, 2:3, :] * mixed
    x1_ref[0] = x1
    h2 = _norm_mod(x1, nw_ref[...], mod_ref[0, 4:5, :], mod_ref[0, 3:4, :])
    for j in range(SUBLANES):
        h2_ref[0, :, j, :] = h2[:, j * LANES:(j + 1) * LANES]
    q = jnp.dot(h2.astype(BF16), wq_ref[...], preferred_element_type=F32)
    for m in range(2 * PEER_HEADS):
        keys = k1_ref[...] if m % 2 == 0 else k2_ref[...]
        qh = q[:, m * PEER_HALF:(m + 1) * PEER_HALF].astype(BF16)
        sc_ref[0, m] = lax.dot_general(keys, qh, (((1,), (1,)), ((), ())), preferred_element_type=F32)


def _mixout(o_hg, o_s5, x, mod6, w_out, nw, w_q, keys1, keys2, tm):
    bsz, seq, d = x.shape
    w = o_hg.shape[2]
    nq = w_q.shape[1]
    wo = w_out.astype(BF16)

    def full(a):
        return pl.BlockSpec(a.shape, lambda b, i: (0,) * a.ndim)

    consts_a = [wo[:w], wo[w:]]
    consts_b = [nw.reshape(1, d), w_q.astype(BF16), keys1.astype(BF16), keys2.astype(BF16)]
    return pl.pallas_call(
        _mixout_kernel,
        grid=(bsz, seq // tm),
        in_specs=[pl.BlockSpec((1, tm, w), lambda b, i: (b, i, 0)),
                  pl.BlockSpec((1, tm, w), lambda b, i: (b, i, 0)),
                  pl.BlockSpec((1, tm, d), lambda b, i: (b, i, 0)),
                  pl.BlockSpec((1, 6, d), lambda b, i: (b, 0, 0))]
                 + [full(a) for a in consts_a] + [full(a) for a in consts_b],
        out_specs=[pl.BlockSpec((1, tm, d), lambda b, i: (b, i, 0)),
                   pl.BlockSpec((1, tm, SUBLANES, LANES), lambda b, i: (b, i, 0, 0)),
                   pl.BlockSpec((1, 2 * PEER_HEADS, PEER_NKEYS, tm), lambda b, i: (b, 0, 0, i))],
        out_shape=(jax.ShapeDtypeStruct((bsz, seq, d), F32),
                   jax.ShapeDtypeStruct((bsz, seq, SUBLANES, LANES), F32),
                   jax.ShapeDtypeStruct((bsz, 2 * PEER_HEADS, PEER_NKEYS, seq), F32)),
        compiler_params=_cparams("arbitrary", "arbitrary", vmem=VMEM_LIMIT),
        name="mixout",
    )(o_hg, o_s5, x, mod6, *consts_a, *consts_b)


def _extract_max(tiles, ids, *payloads):
    lvl = list(zip(tiles, ids, *payloads))
    while len(lvl) > 1:
        nxt = []
        for a, b in zip(lvl[0::2], lvl[1::2]):
            take = a[0] >= b[0]
            nxt.append(tuple(jnp.where(take, x, y) for x, y in zip(a, b)))
        if len(lvl) % 2:
            nxt.append(lvl[-1])
        lvl = nxt
    best, best_id = lvl[0][0], lvl[0][1]
    m = jnp.max(best, axis=0, keepdims=True)
    big = jnp.iinfo(jnp.int32).max
    am = jnp.min(jnp.where(best == m, best_id, big), axis=0, keepdims=True)
    picked = [jnp.max(jnp.where(best_id == am, p, -1), axis=0, keepdims=True) for p in lvl[0][2:]]
    return (m, am, *picked)


def _topk_rows(s, k):
    rows, tt = s.shape
    row = lax.broadcasted_iota(I32, (SUBLANES, tt), 0)
    tiles = [s[j:j + SUBLANES] for j in range(0, rows, SUBLANES)]
    ids = [row + j for j in range(0, rows, SUBLANES)]
    vals, idxs = [], []
    for it in range(k):
        m, am = _extract_max(tiles, ids)
        vals.append(m)
        idxs.append(am)
        if it + 1 < k:
            tiles = [jnp.where(i == am, -jnp.inf, t) for t, i in zip(tiles, ids)]
    return vals, idxs


def _topk_kernel(sc_ref, e_ref, g_ref):
    k = PEER_TOPK
    tt = sc_ref.shape[3]
    assert k == 2 * SUBLANES

    def head(h, carry):
        v1, i1 = _topk_rows(sc_ref[0, 2 * h], k)
        v2, i2 = _topk_rows(sc_ref[0, 2 * h + 1], k)
        v1a = jnp.concatenate(v1, axis=0)
        i1a = jnp.concatenate(i1, axis=0)
        v2a = jnp.concatenate(v2, axis=0)
        i2a = jnp.concatenate(i2, axis=0)
        row = lax.broadcasted_iota(I32, (SUBLANES, tt), 0)
        cands, cidxs, cposs = [], [], []
        tail = k // 2
        for a in range(tail):
            nb = k // (a + 1)
            for b0 in range(0, nb, SUBLANES):
                c = v1[a] + v2a[b0:b0 + SUBLANES]
                if nb - b0 < SUBLANES:
                    c = jnp.where(row < nb - b0, c, -jnp.inf)
                cands.append(c)
                cidxs.append(i1[a] * PEER_NKEYS + i2a[b0:b0 + SUBLANES])
                cposs.append(a * k + b0 + row)
        cands.append(v1a[tail:] + v2[0])
        cidxs.append(i1a[tail:] * PEER_NKEYS + i2[0])
        cposs.append((tail + row) * k)

        best, eidx = [], []
        for it in range(k):
            m, pos, e = _extract_max(cands, cposs, cidxs)
            best.append(m)
            eidx.append(e)
            if it + 1 < k:
                cands = [jnp.where(cp == pos, -jnp.inf, c) for cp, c in zip(cposs, cands)]
        best = jnp.concatenate(best, axis=0)
        ex = jnp.exp(best - best[0:1, :])
        r0 = pl.multiple_of(h * k, k)
        g_ref[pl.ds(r0, k), :] = ex / jnp.sum(ex, axis=0, keepdims=True)
        e_ref[pl.ds(r0, k), :] = jnp.concatenate(eidx, axis=0)
        return carry

    lax.fori_loop(0, PEER_HEADS, head, 0)


def _topk(scores_t, tt):
    bsz, m, nk, seq = scores_t.shape
    rows = PEER_HEADS * PEER_TOPK
    nblk = seq // tt
    out_blk = pl.BlockSpec((rows, tt), lambda b, i: (0, b * nblk + i))
    return pl.pallas_call(
        _topk_kernel,
        grid=(bsz, nblk),
        in_specs=[pl.BlockSpec((1, m, nk, tt), lambda b, i: (b, 0, 0, i))],
        out_specs=[out_blk, out_blk],
        out_shape=(jax.ShapeDtypeStruct((rows, bsz * seq), I32),
                   jax.ShapeDtypeStruct((rows, bsz * seq), F32)),
        compiler_params=_cparams("arbitrary", "arbitrary", vmem=VMEM_LIMIT),
        name="topk",
    )(scores_t)


def _table_tiles(tab):
    e, d = tab.shape
    return tab.astype(BF16).reshape(e, d // LANES, LANES)


def _sublane_sums(p):
    row = lax.broadcasted_iota(I32, (SUBLANES, LANES), 0)
    m4 = row < 4
    m2 = (row & 3) < 2
    m1 = (row & 1) < 1

    def fold4(a, b):
        return jnp.where(m4, a, b) + pltpu.roll(jnp.where(m4, b, a), 4, 0)

    def fold(c1, c2, m, d):
        return (jnp.where(m, c1, pltpu.roll(c2, d, 0))
                + jnp.where(m, pltpu.roll(c1, SUBLANES - d, 0), c2))

    d1 = fold(fold4(p[0], p[4]), fold4(p[2], p[6]), m2, 2)
    d2 = fold(fold4(p[1], p[5]), fold4(p[3], p[7]), m2, 2)
    return fold(d1, d2, m1, 1)


def _stage_ids(idx_hbm, id_refs, sems, tb):
    i = pl.program_id(0)

    def copies(blk, buf):
        return [pltpu.make_async_copy(idx_hbm.at[k, pl.ds(blk * tb, tb)], ref.at[pl.ds(buf * tb, tb)], sems.at[buf])
                for k, ref in enumerate(id_refs)]

    @pl.when(i == 0)
    def _():
        for cp in copies(0, 0):
            cp.start()

    @pl.when(i + 1 < pl.num_programs(0))
    def _():
        for cp in copies(i + 1, (i + 1) % 2):
            cp.start()

    for cp in copies(i, i % 2):
        cp.wait()
    return (i % 2) * tb


def _peer_u_kernel(idx_hbm, h_ref, g_ref, tab_ref, o_ref, acts_s, red_s, sems, *id_refs):
    nk, tb = g_ref.shape
    groups = nk // SUBLANES
    lane = lax.broadcasted_iota(I32, (SUBLANES, tb), 1)
    id0 = _stage_ids(idx_hbm, id_refs, sems, tb)
    acts_s[...] = jnp.zeros_like(acts_s)
    red_s[...] = jnp.zeros_like(red_s)

    def finish(t):
        for g in range(groups):
            rows = slice(g * SUBLANES, (g + 1) * SUBLANES)
            col = jnp.sum(red_s[g], axis=1, keepdims=True)
            acts_s[rows, :] = jnp.where(lane == t, col, acts_s[rows, :])

    def token(t, carry):
        finish(t - 1)
        h = h_ref[t]
        ti = id0 + t
        for g in range(groups):
            p = [tab_ref[id_refs[g * SUBLANES + i][ti]].astype(F32) * h for i in range(SUBLANES)]
            red_s[g] = _sublane_sums(p)
        return carry

    lax.fori_loop(0, tb, token, 0)
    finish(tb - 1)
    o_ref[...] = g_ref[...] * _gelu(acts_s[...])


def _peer_u(eidx, h2r, gates, tab, tb):
    nk, nt = gates.shape
    return pl.pallas_call(
        _peer_u_kernel,
        grid=(nt // tb,),
        in_specs=[pl.BlockSpec(memory_space=pl.ANY),
                  pl.BlockSpec((tb, SUBLANES, LANES), lambda i: (i, 0, 0)),
                  pl.BlockSpec((nk, tb), lambda i: (0, i)),
                  pl.BlockSpec(tab.shape, lambda i: (0, 0, 0), pipeline_mode=pl.Buffered(1))],
        out_specs=pl.BlockSpec((nk, tb), lambda i: (0, i)),
        out_shape=jax.ShapeDtypeStruct((nk, nt), F32),
        scratch_shapes=[pltpu.VMEM((nk, tb), F32), pltpu.VMEM((nk // SUBLANES, SUBLANES, LANES), F32),
                        pltpu.SemaphoreType.DMA((2,))] + [pltpu.SMEM((2 * tb,), I32)] * nk,
        compiler_params=_cparams("arbitrary", vmem=VMEM_LIMIT),
        name="peer_u",
    )(eidx, h2r, gates, tab)


def _peer_v_kernel(idx_hbm, c_ref, tab_ref, o_ref, cb0_s, cb1_s, part_s, sems, *id_refs):
    nk, tb = c_ref.shape
    groups = nk // SUBLANES
    lane = lax.broadcasted_iota(I32, (SUBLANES, tb), 1)
    id0 = _stage_ids(idx_hbm, id_refs, sems, tb)

    def prep(t, cb):
        for g in range(groups):
            rows = slice(g * SUBLANES, (g + 1) * SUBLANES)
            col = jnp.sum(jnp.where(lane == t, c_ref[rows, :], 0.0), axis=1, keepdims=True)
            cb[rows, :] = jnp.broadcast_to(col, (SUBLANES, LANES))

    def gather(t, cb):
        ti = id0 + t
        for g in range(groups):
            p = [cb[k:k + 1, :] * tab_ref[id_refs[k][ti]].astype(F32)
                 for k in range(g * SUBLANES, (g + 1) * SUBLANES)]
            part_s[g] = ((p[0] + p[1]) + (p[2] + p[3])) + ((p[4] + p[5]) + (p[6] + p[7]))

    def finish(t):
        s = [part_s[g] for g in range(groups)]
        while len(s) > 1:
            s = [s[i] + s[i + 1] for i in range(0, len(s), 2)]
        o_ref[t] = s[0]

    prep(0, cb0_s)
    part_s[...] = jnp.zeros_like(part_s)

    def token(t, carry):
        prep(t + 1, cb1_s)
        finish(jnp.maximum(t - 1, 0))
        gather(t, cb0_s)
        cb0_s[...] = cb1_s[...]
        return carry

    lax.fori_loop(0, tb, token, 0)
    finish(tb - 1)


def _peer_v(eidx, coef, tab, tb):
    nk, nt = coef.shape
    return pl.pallas_call(
        _peer_v_kernel,
        grid=(nt // tb,),
        in_specs=[pl.BlockSpec(memory_space=pl.ANY),
                  pl.BlockSpec((nk, tb), lambda i: (0, i)),
                  pl.BlockSpec(tab.shape, lambda i: (0, 0, 0), pipeline_mode=pl.Buffered(1))],
        out_specs=pl.BlockSpec((tb, SUBLANES, LANES), lambda i: (i, 0, 0)),
        out_shape=jax.ShapeDtypeStruct((nt, SUBLANES, LANES), F32),
        scratch_shapes=[pltpu.VMEM((nk, LANES), F32), pltpu.VMEM((nk, LANES), F32),
                        pltpu.VMEM((nk // SUBLANES, SUBLANES, LANES), F32),
                        pltpu.SemaphoreType.DMA((2,))] + [pltpu.SMEM((2 * tb,), I32)] * nk,
        compiler_params=_cparams("arbitrary", vmem=VMEM_LIMIT),
        name="peer_v",
    )(eidx, coef, tab)


def _final_kernel(last, x_ref, p_ref, mod_ref, nw_ref, o_ref):
    peer = jnp.concatenate([p_ref[0, :, j, :] for j in range(SUBLANES)], axis=1)
    x2 = x_ref[0] + mod_ref[0, 5:6, :] * peer
    if last:
        ms = jnp.mean(x2 * x2, axis=-1, keepdims=True)
        x2 = x2 * lax.rsqrt(ms + EPS) * nw_ref[...]
    o_ref[0] = x2


def _final(x1, peer_tiles, mod6, nw, last, tm):
    bsz, seq, d = x1.shape
    blk = pl.BlockSpec((1, tm, d), lambda b, i: (b, i, 0))
    return pl.pallas_call(
        functools.partial(_final_kernel, last),
        grid=(bsz, seq // tm),
        in_specs=[blk, pl.BlockSpec((1, tm, SUBLANES, LANES), lambda b, i: (b, i, 0, 0)),
                  pl.BlockSpec((1, 6, d), lambda b, i: (b, 0, 0)),
                  pl.BlockSpec((1, d), lambda b, i: (0, 0))],
        out_specs=blk,
        out_shape=jax.ShapeDtypeStruct((bsz, seq, d), F32),
        compiler_params=_cparams("arbitrary", "arbitrary", vmem=VMEM_LIMIT),
        name="final",
    )(x1, peer_tiles, mod6, nw.reshape(1, d))


def _tile(n, pref):
    t = min(n, pref)
    assert n % t == 0, (n, t)
    return t


def _token_tiles(seq, nt):
    return {
        "inproj": _tile(seq, 512),
        "hgrn2": _tile(seq, 512),
        "s5": _tile(seq, 256),
        "mixout": _tile(seq, 512),
        "topk": _tile(seq, 512),
        "peer": _tile(nt, 128),
        "peer_v": _tile(nt, 256),
        "final": _tile(seq, 512),
    }


def kernel(x, c, ada_w, ada_b, norm_mix_w, norm_ffn_w, w_in, w_out, hg_lower_bounds, hg_gnorm_w,
           s5_a_re, s5_a_im, s5_log_dt, s5_b_re, s5_b_im, s5_c_re, s5_c_im, s5_d, s5_glu_w, s5_glu_b,
           peer_wq, peer_keys1, peer_keys2, peer_u, peer_v, final_norm_w):
    bsz, seq, d = x.shape
    depth = ada_w.shape[0]
    nt = bsz * seq
    hg_w = HG_HEADS * HG_DK
    assert d == SUBLANES * LANES and seq % HG_CHUNK == 0
    assert w_in.shape[2] == 4 * hg_w + s5_a_re.shape[1] * S5_GROUP and 4 * hg_w % (s5_a_re.shape[1] * S5_GROUP) == 0
    s5_col = 4 * hg_w // (s5_a_re.shape[1] * S5_GROUP)

    tiles = _token_tiles(seq, nt)
    for l in range(depth):
        mod6 = _mod(c, ada_w[l], ada_b[l]).reshape(bsz, 6, d)
        proj = _inproj(x, mod6, norm_mix_w[l], w_in[l].astype(BF16), tiles["inproj"])
        o_hg = _hgrn2(proj, hg_lower_bounds, hg_gnorm_w[l], l, tiles["hgrn2"])
        o_s5 = _s5(proj, s5_col, s5_a_re[l], s5_a_im[l], s5_log_dt[l], s5_b_re[l], s5_b_im[l],
                   s5_c_re[l], s5_c_im[l], s5_d[l], s5_glu_w[l], s5_glu_b[l], tiles["s5"])
        x1, h2_tiles, scores_t = _mixout(o_hg, o_s5, x, mod6, w_out[l], norm_ffn_w[l], peer_wq[l],
                                         peer_keys1[l], peer_keys2[l], tiles["mixout"])
        eidx, gates = _topk(scores_t, tiles["topk"])
        coef = _peer_u(eidx, h2_tiles.reshape(nt, SUBLANES, LANES), gates, _table_tiles(peer_u[l]), tiles["peer"])
        peer_tiles = _peer_v(eidx, coef, _table_tiles(peer_v[l]), tiles["peer_v"])
        x = _final(x1, peer_tiles.reshape(bsz, seq, SUBLANES, LANES), mod6, final_norm_w, l == depth - 1,
                   tiles["final"])
    return x
```
